```python
import math
import jax
import jax.numpy as jnp
from jax import lax
import numpy as np

D_MODEL = 1024
BATCH = 32
SEQ = 2048
DEPTH = 4

GRID_W = 64
CTX_LEN = 256
EPS = 1e-6
ROPE_THETA = 10000.0
Q_BLOCK = 128
MIX_HALF = D_MODEL // 2

A_HEAD_DIM = 64
A_HEADS = MIX_HALF // A_HEAD_DIM
A_KV_HEADS = A_HEADS // 4
A_GROUP = A_HEADS // A_KV_HEADS
B_NOPE_DIM = 64
B_ROPE_DIM = 32
B_V_DIM = 64
B_HEADS = MIX_HALF // B_V_DIM
B_Q_RANK = D_MODEL // 4
B_KV_RANK = D_MODEL // 8
C_WIDTH = MIX_HALF
C_ORDER = 2
C_SHORT_K = 3
C_POS_EMB = 33
C_FILTER_HIDDEN = 64
C_DECAY_TARGET = 1e-2
C_DECAY_FRAC_SHORT = 0.3
C_DECAY_FRAC_LONG = 1.5
D_INNER = MIX_HALF
D_HEAD_DIM = 64
D_HEADS = D_INNER // D_HEAD_DIM
D_GROUPS = 2
D_STATE = 128
D_CONV_K = 3
D_CHUNK = 128
FF_HIDDEN = 4 * D_MODEL

A_SCALE = A_HEAD_DIM ** -0.5
B_SCALE = (B_NOPE_DIM + B_ROPE_DIM) ** -0.5

EVEN_KV_SIZES = (A_KV_HEADS * A_HEAD_DIM, A_KV_HEADS * A_HEAD_DIM, B_KV_RANK, B_ROPE_DIM)
EVEN_Q_SIZES = (A_HEADS * A_HEAD_DIM, B_Q_RANK)
EVEN_KV_COLS = sum(EVEN_KV_SIZES)
EVEN_IN = EVEN_KV_COLS + sum(EVEN_Q_SIZES)
EVEN_OUT = A_HEADS * A_HEAD_DIM + B_HEADS * B_V_DIM
ODD_HY = (C_ORDER + 1) * C_WIDTH
ODD_Z = D_INNER
ODD_XBC = D_INNER + 2 * D_GROUPS * D_STATE
ODD_DT = 2 * D_HEADS
ODD_IN = ODD_HY + ODD_Z + ODD_XBC + ODD_DT
ODD_CTX_START = ODD_HY + ODD_Z
ODD_OUT = C_WIDTH + D_INNER

kernel_name = 'hybrid_dit_gqa_mla_hyena_ssd'


def rms_norm(x, g):
    xf = x.astype(jnp.float32)
    y = xf * lax.rsqrt(jnp.mean(xf * xf, axis=-1, keepdims=True) + EPS)
    return (y * g.astype(jnp.float32)).astype(x.dtype)


def modulate(h, shift, scale):
    return h * (1.0 + scale) + shift


def split_cols(p, sizes):
    return jnp.split(p, np.cumsum(sizes)[:-1].tolist(), axis=-1)


def centred_dwconv(u, w, b):
    k, ch = w.shape
    r = k // 2
    y = lax.conv_general_dilated(u, w[:, None, :].astype(u.dtype), window_strides=(1,),
                                 padding=[(r, r)], dimension_numbers=('NWC', 'WIO', 'NWC'),
                                 feature_group_count=ch)
    return y + b


def sq_relu_mlp(h, w1, w2):
    return jnp.square(jax.nn.relu(h @ w1)) @ w2


def axial_rope(n_tokens, rot_dim):
    rows = n_tokens // GRID_W
    r_idx, c_idx = jnp.meshgrid(jnp.arange(rows), jnp.arange(GRID_W), indexing='ij')
    quarter = rot_dim // 4
    inv_freq = ROPE_THETA ** (-jnp.arange(quarter, dtype=jnp.float32) / quarter)
    ang = jnp.concatenate([r_idx.reshape(-1, 1).astype(jnp.float32) * inv_freq,
                           c_idx.reshape(-1, 1).astype(jnp.float32) * inv_freq], axis=-1)
    return jnp.cos(ang), jnp.sin(ang)


def apply_rope(t, cos, sin):
    half = t.shape[-1] // 2
    t1 = t[..., :half].astype(jnp.float32)
    t2 = t[..., half:].astype(jnp.float32)
    return jnp.concatenate([t1 * cos - t2 * sin, t2 * cos + t1 * sin], axis=-1).astype(t.dtype)


def heads_first(t):
    return t.transpose(0, 2, 1, 3)


def attend(q, k, v, scale):
    s = jnp.einsum('bhgqd,bhkd->bhgqk', q, k).astype(jnp.float32) * scale
    p = jax.nn.softmax(s, axis=-1).astype(v.dtype)
    return jnp.einsum('bhgqk,bhkd->bhgqd', p, v)


def blocked_attend(q, k, v, scale):
    b, hk, g, L, dk = q.shape
    nb = L // Q_BLOCK
    qb = jnp.moveaxis(q.reshape(b, hk, g, nb, Q_BLOCK, dk), 3, 0)
    o = lax.map(lambda qi: attend(qi, k, v, scale), qb)
    return jnp.moveaxis(o, 0, 3).reshape(b, hk, g, L, v.shape[-1])


def merge_heads(o):
    b, hk, g, L, d = o.shape
    return o.transpose(0, 3, 1, 2, 4).reshape(b, L, hk * g * d)


def even_kv(p_kv, a_kn, b_kvn, b_wkv, rope_a, rope_b):
    bsz, L, _ = p_kv.shape
    ak, av, bkva, bkr = split_cols(p_kv, EVEN_KV_SIZES)
    ak = heads_first(rms_norm(ak.reshape(bsz, L, A_KV_HEADS, A_HEAD_DIM), a_kn))
    av = heads_first(av.reshape(bsz, L, A_KV_HEADS, A_HEAD_DIM))
    bkv = heads_first((rms_norm(bkva, b_kvn) @ b_wkv).reshape(bsz, L, B_HEADS, B_NOPE_DIM + B_V_DIM))
    bkr = bkr[:, None]
    if rope_a is not None:
        ak = apply_rope(ak, *rope_a)
        bkr = apply_rope(bkr, *rope_b)
    bk = jnp.concatenate([bkv[..., :B_NOPE_DIM],
                          jnp.broadcast_to(bkr, (bsz, B_HEADS, L, B_ROPE_DIM))], axis=-1)
    return ak, av, bk, bkv[..., B_NOPE_DIM:]


def even_q(p_q, a_qn, b_qn, b_wq, rope_a, rope_b):
    bsz, L, _ = p_q.shape
    aq, bqa = split_cols(p_q, EVEN_Q_SIZES)
    aq = heads_first(rms_norm(aq.reshape(bsz, L, A_HEADS, A_HEAD_DIM), a_qn))
    bq = heads_first((rms_norm(bqa, b_qn) @ b_wq).reshape(bsz, L, B_HEADS, B_NOPE_DIM + B_ROPE_DIM))
    if rope_a is not None:
        aq = apply_rope(aq, *rope_a)
        bq = jnp.concatenate([bq[..., :B_NOPE_DIM], apply_rope(bq[..., B_NOPE_DIM:], *rope_b)], axis=-1)
    return aq.reshape(bsz, A_KV_HEADS, A_GROUP, L, A_HEAD_DIM), bq[:, :, None]


def attention_mixer(hx, hc, w_in, w_out, a_qn, a_kn, b_qn, b_kvn, b_wq, b_wkv, need_ctx):
    L = hx.shape[1]
    rope_a = axial_rope(L, A_HEAD_DIM)
    rope_b = axial_rope(L, B_ROPE_DIM)
    px = hx @ w_in
    ka_x, va_x, kb_x, vb_x = even_kv(px[..., :EVEN_KV_COLS], a_kn, b_kvn, b_wkv, rope_a, rope_b)
    qa_x, qb_x = even_q(px[..., EVEN_KV_COLS:], a_qn, b_qn, b_wq, rope_a, rope_b)
    pc = hc @ (w_in if need_ctx else w_in[:, :EVEN_KV_COLS])
    ka_c, va_c, kb_c, vb_c = even_kv(pc[..., :EVEN_KV_COLS], a_kn, b_kvn, b_wkv, None, None)
    cat = lambda s, t: jnp.concatenate([s, t], axis=2)
    oa = blocked_attend(qa_x, cat(ka_x, ka_c), cat(va_x, va_c), A_SCALE)
    ob = blocked_attend(qb_x, cat(kb_x, kb_c), cat(vb_x, vb_c), B_SCALE)
    out_x = jnp.concatenate([merge_heads(oa), merge_heads(ob)], axis=-1) @ w_out
    if not need_ctx:
        return out_x, None
    qa_c, qb_c = even_q(pc[..., EVEN_KV_COLS:], a_qn, b_qn, b_wq, None, None)
    oa_c = attend(qa_c, ka_c, va_c, A_SCALE)
    ob_c = attend(qb_c, kb_c, vb_c, B_SCALE)
    out_c = jnp.concatenate([merge_heads(oa_c), merge_heads(ob_c)], axis=-1) @ w_out
    return out_x, out_c


def hyena_filter_spectrum(n_tokens, w1, b1, w2, b2, w3, freq):
    f32 = jnp.float32
    L = n_tokens
    t = jnp.linspace(0.0, 1.0, L, dtype=f32)
    bands = (C_POS_EMB - 1) // 2
    w = 2.0 * math.pi * jnp.arange(L, dtype=f32) / L
    fb = jnp.linspace(1e-4, bands - 1, bands, dtype=f32)
    ph = w[:, None] * fb[None, :]
    z = jnp.concatenate([t[:, None], jnp.cos(ph), -jnp.sin(ph)], axis=-1)
    fr = freq.astype(f32)
    h = jnp.sin(fr * (z @ w1.astype(f32) + b1.astype(f32)))
    h = jnp.sin(fr * (h @ w2.astype(f32) + b2.astype(f32)))
    h = (h @ w3.astype(f32)).reshape(L, C_ORDER, 2, C_WIDTH)
    decay_max = math.log(C_DECAY_TARGET) / C_DECAY_FRAC_SHORT
    decay_min = math.log(C_DECAY_TARGET) / C_DECAY_FRAC_LONG
    deltas = jnp.abs(jnp.linspace(decay_min, decay_max, C_WIDTH, dtype=f32))
    h = h * jnp.exp(-t[:, None] * deltas[None, :])[:, None, None, :]
    fwd, bwd = h[:, :, 0], h[:, :, 1]
    two_sided = jnp.concatenate([fwd, jnp.zeros_like(fwd[:1]), bwd[:0:-1]], axis=0)
    return jnp.fft.rfft(two_sided, axis=0)


def fft_long_conv(u, spec, skip):
    L = u.shape[1]
    uf = jnp.fft.rfft(u.astype(jnp.float32), n=2 * L, axis=1)
    y = jnp.fft.irfft(uf * spec[None], n=2 * L, axis=1)[:, :L]
    return (y + u.astype(jnp.float32) * skip.astype(jnp.float32)).astype(u.dtype)


def hyena_sequence(u3, short_w, short_b, spec, skip):
    uc = centred_dwconv(u3, short_w, short_b)
    parts = jnp.split(uc, C_ORDER + 1, axis=-1)
    z = parts[0]
    for o in range(C_ORDER):
        z = parts[o + 1] * fft_long_conv(z, spec[:, o], skip[o])
    return z


def segsum(a):
    T = a.shape[-1]
    cs = jnp.cumsum(a, axis=-1)
    diff = cs[..., :, None] - cs[..., None, :]
    mask = jnp.tril(jnp.ones((T, T), dtype=bool))
    return jnp.where(mask, diff, -jnp.inf)


def ssd_chunked(xs, dt, a, bm, cm, init_state, want_y):
    f32 = jnp.float32
    bsz, L, H, P = xs.shape
    G, N = bm.shape[2], bm.shape[3]
    E = H // G
    nc = L // D_CHUNK
    x = (xs.astype(f32) * dt[..., None]).reshape(bsz, nc, D_CHUNK, G, E, P)
    adt = (dt * a).reshape(bsz, nc, D_CHUNK, G, E).transpose(0, 3, 4, 1, 2)
    bc = bm.astype(f32).reshape(bsz, nc, D_CHUNK, G, N)
    cc = cm.astype(f32).reshape(bsz, nc, D_CHUNK, G, N)
    acs = jnp.cumsum(adt, axis=-1)
    decay_states = jnp.exp(acs[..., -1:] - acs)
    states = jnp.einsum('bclgn,bgecl,bclgep->bcgepn', bc, decay_states, x)
    states = jnp.concatenate([init_state[:, None], states], axis=1)
    chunk_decay = jnp.exp(segsum(jnp.pad(acs[..., -1], ((0, 0), (0, 0), (0, 0), (1, 0)))))
    states = jnp.einsum('bgezc,bcgepn->bzgepn', chunk_decay, states)
    final = states[:, -1]
    if not want_y:
        return None, final
    wdiag = jnp.einsum('bclgn,bcsgn->bgcls', cc, bc)[:, :, None] * jnp.exp(segsum(adt))
    y_diag = jnp.einsum('bgecls,bcsgep->bclgep', wdiag, x)
    y_off = jnp.einsum('bclgn,bcgepn,bgecl->bclgep', cc, states[:, :-1], jnp.exp(acs))
    return (y_diag + y_off).reshape(bsz, L, H, P), final


def mamba_inputs(p, conv_w, conv_b, dt_bias):
    bsz, L, _ = p.shape
    xbc, dt = split_cols(p, (ODD_XBC, ODD_DT))
    xbc = jax.nn.silu(centred_dwconv(xbc, conv_w, conv_b))
    xs, bm, cm = split_cols(xbc, (D_INNER, D_GROUPS * D_STATE, D_GROUPS * D_STATE))
    dt = jax.nn.softplus(dt.astype(jnp.float32).reshape(bsz, L, 2, D_HEADS) + dt_bias.astype(jnp.float32))
    return (xs.reshape(bsz, L, D_HEADS, D_HEAD_DIM), bm.reshape(bsz, L, D_GROUPS, D_STATE),
            cm.reshape(bsz, L, D_GROUPS, D_STATE), dt)


def bidir_ssd(lat, cx, a_log, need_ctx):
    xs_x, b_x, c_x, dt_x = lat
    xs_c, b_c, c_c, dt_c = cx
    bsz = xs_x.shape[0]
    state0 = jnp.zeros((bsz, D_GROUPS, D_HEADS // D_GROUPS, D_HEAD_DIM, D_STATE), jnp.float32)
    outs_x, outs_c = [], []
    for d in range(2):
        a = -jnp.exp(a_log[d].astype(jnp.float32))
        orient = (lambda t: t) if d == 0 else (lambda t: jnp.flip(t, 1))
        yc, s_c = ssd_chunked(orient(xs_c), orient(dt_c[:, :, d]), a, orient(b_c), orient(c_c), state0, need_ctx)
        yx, _ = ssd_chunked(orient(xs_x), orient(dt_x[:, :, d]), a, orient(b_x), orient(c_x), s_c, True)
        outs_x.append(orient(yx))
        if need_ctx:
            outs_c.append(orient(yc))
    y_c = outs_c[0] + outs_c[1] if need_ctx else None
    return outs_x[0] + outs_x[1], y_c


def mamba_out(y, xs, z, d_skip, norm_g):
    bsz, L = z.shape[0], z.shape[1]
    y = (y + xs.astype(jnp.float32) * d_skip.astype(jnp.float32)[:, None]).reshape(bsz, L, D_INNER)
    return rms_norm(y.astype(z.dtype) * jax.nn.silu(z), norm_g)


def long_conv_ssd_mixer(hx, hc, w_in, w_out, hy_short_w, hy_short_b, hy_w1, hy_b1, hy_w2, hy_b2,
                        hy_w3, hy_freq, hy_skip, mb_conv_w, mb_conv_b, mb_a_log, mb_dt_bias,
                        mb_d_skip, mb_norm, need_ctx):
    px = hx @ w_in
    hy_x, z_x, m_x = split_cols(px, (ODD_HY, ODD_Z, ODD_XBC + ODD_DT))
    spec_x = hyena_filter_spectrum(hx.shape[1], hy_w1, hy_b1, hy_w2, hy_b2, hy_w3, hy_freq)
    yh_x = hyena_sequence(hy_x, hy_short_w, hy_short_b, spec_x, hy_skip)
    lat = mamba_inputs(m_x, mb_conv_w, mb_conv_b, mb_dt_bias)
    pc = hc @ (w_in if need_ctx else w_in[:, ODD_CTX_START:])
    if need_ctx:
        hy_c, z_c, m_c = split_cols(pc, (ODD_HY, ODD_Z, ODD_XBC + ODD_DT))
    else:
        m_c = pc
    cx = mamba_inputs(m_c, mb_conv_w, mb_conv_b, mb_dt_bias)
    ym_x, ym_c = bidir_ssd(lat, cx, mb_a_log, need_ctx)
    out_x = jnp.concatenate([yh_x, mamba_out(ym_x, lat[0], z_x, mb_d_skip, mb_norm)], axis=-1) @ w_out
    if not need_ctx:
        return out_x, None
    spec_c = hyena_filter_spectrum(hc.shape[1], hy_w1, hy_b1, hy_w2, hy_b2, hy_w3, hy_freq)
    yh_c = hyena_sequence(hy_c, hy_short_w, hy_short_b, spec_c, hy_skip)
    out_c = jnp.concatenate([yh_c, mamba_out(ym_c, cx[0], z_c, mb_d_skip, mb_norm)], axis=-1) @ w_out
    return out_x, out_c


def setup_inputs(seed: int = 0) -> dict:
    key = jax.random.key(seed)
    ks = iter(jax.random.split(key, 64))
    f32 = jnp.float32
    ne = (DEPTH + 1) // 2
    no = DEPTH // 2

    def nrm(shape, fan_in, scale=1.0):
        return scale * fan_in ** -0.5 * jax.random.normal(next(ks), shape, f32)

    def gain(shape):
        return 1.0 + 0.05 * jax.random.normal(next(ks), shape, f32)

    def small(shape):
        return 0.02 * jax.random.normal(next(ks), shape, f32)

    x = jax.random.normal(next(ks), (BATCH, SEQ, D_MODEL), f32)
    c = jax.random.normal(next(ks), (BATCH, D_MODEL), f32)
    ctx = jax.random.normal(next(ks), (BATCH, CTX_LEN, D_MODEL), f32)
    c_ctx = jax.random.normal(next(ks), (D_MODEL,), f32)
    ada_w = nrm((DEPTH, D_MODEL, 6 * D_MODEL), D_MODEL, 0.5)
    ada_b = small((DEPTH, 6 * D_MODEL))
    norm_g = gain((DEPTH, 4, D_MODEL))
    ff_w1 = nrm((DEPTH, D_MODEL, FF_HIDDEN), D_MODEL)
    ff_w2 = nrm((DEPTH, FF_HIDDEN, D_MODEL), FF_HIDDEN)
    at_w_in = nrm((ne, D_MODEL, EVEN_IN), D_MODEL)
    at_w_out = nrm((ne, EVEN_OUT, D_MODEL), EVEN_OUT)
    ga_q_norm = gain((ne, A_HEAD_DIM))
    ga_k_norm = gain((ne, A_HEAD_DIM))
    ml_q_norm = gain((ne, B_Q_RANK))
    ml_kv_norm = gain((ne, B_KV_RANK))
    ml_w_q_up = nrm((ne, B_Q_RANK, B_HEADS * (B_NOPE_DIM + B_ROPE_DIM)), B_Q_RANK)
    ml_w_kv_up = nrm((ne, B_KV_RANK, B_HEADS * (B_NOPE_DIM + B_V_DIM)), B_KV_RANK)
    lc_w_in = nrm((no, D_MODEL, ODD_IN), D_MODEL)
    lc_w_out = nrm((no, ODD_OUT, D_MODEL), ODD_OUT)
    hy_short_w = nrm((no, C_SHORT_K, ODD_HY), C_SHORT_K)
    hy_short_b = small((no, ODD_HY))
    hy_w1 = nrm((no, C_POS_EMB, C_FILTER_HIDDEN), C_POS_EMB)
    hy_b1 = nrm((no, C_FILTER_HIDDEN), C_POS_EMB)
    hy_w2 = nrm((no, C_FILTER_HIDDEN, C_FILTER_HIDDEN), C_FILTER_HIDDEN)
    hy_b2 = nrm((no, C_FILTER_HIDDEN), C_FILTER_HIDDEN)
    hy_w3 = nrm((no, C_FILTER_HIDDEN, C_ORDER * 2 * C_WIDTH), C_FILTER_HIDDEN, 0.1)
    hy_freq = gain((no, C_FILTER_HIDDEN))
    hy_skip = jax.random.normal(next(ks), (no, C_ORDER, C_WIDTH), f32)
    mb_conv_w = nrm((no, D_CONV_K, ODD_XBC), D_CONV_K)
    mb_conv_b = small((no, ODD_XBC))
    mb_a_log = jnp.log(jax.random.uniform(next(ks), (no, 2, D_HEADS), f32, 1.0, 16.0))
    dt0 = jnp.exp(jax.random.uniform(next(ks), (no, 2, D_HEADS), f32, math.log(1e-3), math.log(1e-1)))
    mb_dt_bias = dt0 + jnp.log(-jnp.expm1(-dt0))
    mb_d_skip = gain((no, D_HEADS))
    mb_norm = gain((no, D_INNER))
    return {'x': x, 'c': c, 'ctx': ctx, 'c_ctx': c_ctx, 'ada_w': ada_w, 'ada_b': ada_b,
            'norm_g': norm_g, 'ff_w1': ff_w1, 'ff_w2': ff_w2, 'at_w_in': at_w_in, 'at_w_out': at_w_out,
            'ga_q_norm': ga_q_norm, 'ga_k_norm': ga_k_norm, 'ml_q_norm': ml_q_norm,
            'ml_kv_norm': ml_kv_norm, 'ml_w_q_up': ml_w_q_up, 'ml_w_kv_up': ml_w_kv_up,
            'lc_w_in': lc_w_in, 'lc_w_out': lc_w_out, 'hy_short_w': hy_short_w, 'hy_short_b': hy_short_b,
            'hy_w1': hy_w1, 'hy_b1': hy_b1, 'hy_w2': hy_w2, 'hy_b2': hy_b2, 'hy_w3': hy_w3,
            'hy_freq': hy_freq, 'hy_skip': hy_skip, 'mb_conv_w': mb_conv_w, 'mb_conv_b': mb_conv_b,
            'mb_a_log': mb_a_log, 'mb_dt_bias': mb_dt_bias, 'mb_d_skip': mb_d_skip, 'mb_norm': mb_norm}


def reference(x, c, ctx, c_ctx, ada_w, ada_b, norm_g, ff_w1, ff_w2, at_w_in, at_w_out,
              ga_q_norm, ga_k_norm, ml_q_norm, ml_kv_norm, ml_w_q_up, ml_w_kv_up,
              lc_w_in, lc_w_out, hy_short_w, hy_short_b, hy_w1, hy_b1, hy_w2, hy_b2, hy_w3,
              hy_freq, hy_skip, mb_conv_w, mb_conv_b, mb_a_log, mb_dt_bias, mb_d_skip, mb_norm):
    cond_x = jax.nn.silu(c)[:, None, :]
    cond_c = jax.nn.silu(c_ctx)
    for i in range(DEPTH):
        need_ctx = i < DEPTH - 1
        j = i // 2
        g = norm_g[i]
        mx = jnp.split(cond_x @ ada_w[i] + ada_b[i], 6, axis=-1)
        mc = jnp.split(cond_c @ ada_w[i] + ada_b[i], 6, axis=-1)
        hx = modulate(rms_norm(x, g[0]), mx[0], mx[1])
        hc = modulate(rms_norm(ctx, g[0]), mc[0], mc[1])
        if i % 2 == 0:
            ox, oc = attention_mixer(hx, hc, at_w_in[j], at_w_out[j], ga_q_norm[j], ga_k_norm[j],
                                     ml_q_norm[j], ml_kv_norm[j], ml_w_q_up[j], ml_w_kv_up[j], need_ctx)
        else:
            ox, oc = long_conv_ssd_mixer(hx, hc, lc_w_in[j], lc_w_out[j], hy_short_w[j], hy_short_b[j],
                                         hy_w1[j], hy_b1[j], hy_w2[j], hy_b2[j], hy_w3[j], hy_freq[j],
                                         hy_skip[j], mb_conv_w[j], mb_conv_b[j], mb_a_log[j],
                                         mb_dt_bias[j], mb_d_skip[j], mb_norm[j], need_ctx)
        x = x + mx[2] * rms_norm(ox, g[1])
        x = x + mx[5] * rms_norm(sq_relu_mlp(modulate(rms_norm(x, g[2]), mx[3], mx[4]),
                                             ff_w1[i], ff_w2[i]), g[3])
        if need_ctx:
            ctx = ctx + mc[2] * rms_norm(oc, g[1])
            ctx = ctx + mc[5] * rms_norm(sq_relu_mlp(modulate(rms_norm(ctx, g[2]), mc[3], mc[4]),
                                                     ff_w1[i], ff_w2[i]), g[3])
    return x
```

```python
import functools
import math

import jax
import jax.numpy as jnp
from jax import lax
from jax.experimental import pallas as pl
from jax.experimental.pallas import tpu as pltpu

F32 = jnp.float32
BF16 = jnp.bfloat16

D_MODEL = 1024
GRID_W = 64
EPS = 1e-6
ROPE_THETA = 10000.0
MIX_HALF = D_MODEL // 2
A_HEAD_DIM = 64
A_HEADS = MIX_HALF // A_HEAD_DIM
A_KV_HEADS = A_HEADS // 4
A_GROUP = A_HEADS // A_KV_HEADS
B_NOPE_DIM = 64
B_ROPE_DIM = 32
B_V_DIM = 64
B_HEADS = MIX_HALF // B_V_DIM
B_Q_RANK = D_MODEL // 4
B_KV_RANK = D_MODEL // 8
B_PAD_DIM = 128
C_WIDTH = MIX_HALF
C_ORDER = 2
C_POS_EMB = 33
C_FILTER_HIDDEN = 64
C_DECAY_TARGET = 1e-2
C_DECAY_FRAC_SHORT = 0.3
C_DECAY_FRAC_LONG = 1.5
D_INNER = MIX_HALF
D_HEAD_DIM = 64
D_HEADS = D_INNER // D_HEAD_DIM
D_GROUPS = 2
D_STATE = 128
D_CHUNK = 128
FF_HIDDEN = 4 * D_MODEL
FF_CHUNK = 1024
A_SCALE = A_HEAD_DIM ** -0.5
B_SCALE = (B_NOPE_DIM + B_ROPE_DIM) ** -0.5
ODD_HY = (C_ORDER + 1) * C_WIDTH
ODD_Z = D_INNER
ODD_XBC = D_INNER + 2 * D_GROUPS * D_STATE
ODD_DT = 2 * D_HEADS
LANE = 128
VMEM_LIMIT = 56 * 2 ** 20


def _params(*sem):
    return pltpu.CompilerParams(dimension_semantics=sem, vmem_limit_bytes=VMEM_LIMIT)


def _const_spec(shape):
    nd = len(shape)
    return pl.BlockSpec(shape, lambda *_: (0,) * nd, pipeline_mode=pl.Buffered(1))


def _dot(a, b):
    return jnp.dot(a, b, preferred_element_type=F32)


def _dot_nt(a, b):
    return lax.dot_general(a, b, (((1,), (1,)), ((), ())), preferred_element_type=F32)


def _dot_f32(a, b):
    return jnp.dot(a, b, preferred_element_type=F32, precision=lax.Precision.HIGHEST)


def _rms_rows(x):
    return x * lax.rsqrt(jnp.mean(x * x, axis=-1, keepdims=True) + EPS)


def _silu(x):
    return x * jax.nn.sigmoid(x)


def _swap_halves(x, half):
    n = x.shape[-1]
    lane = lax.broadcasted_iota(jnp.int32, x.shape, 1)
    r1 = pltpu.roll(x, half, 1)
    r2 = pltpu.roll(x, n - half, 1)
    i1 = pltpu.roll(lane, half, 1)
    return jnp.where(i1 == (lane ^ half), r1, r2)


def _row_neighbours(u):
    n = u.shape[0]
    row = lax.broadcasted_iota(jnp.int32, u.shape, 0)
    r1 = pltpu.roll(u, 1, 0)
    r2 = pltpu.roll(u, n - 1, 0)
    i1 = pltpu.roll(row, 1, 0)
    i2 = pltpu.roll(row, n - 1, 0)
    zero = jnp.zeros_like(u)
    prev = jnp.where(i1 == row - 1, r1, jnp.where(i2 == row - 1, r2, zero))
    nxt = jnp.where(i1 == row + 1, r1, jnp.where(i2 == row + 1, r2, zero))
    return prev, nxt


def _short_conv(u, w_ref, b_ref):
    prev, nxt = _row_neighbours(u)
    return w_ref[0:1, :] * prev + w_ref[1:2, :] * u + w_ref[2:3, :] * nxt + b_ref[...]


def _ada_body(cond_ref, w_ref, b_ref, o_ref):
    h = _silu(cond_ref[...]).astype(BF16)
    o_ref[0] = _dot(h, w_ref[0].astype(BF16)) + b_ref[0]


def _ada_table(cond, ada_w, ada_b):
    depth, d, n = ada_w.shape
    r = cond.shape[0]
    tn = 1536
    return pl.pallas_call(
        _ada_body,
        grid=(depth, n // tn),
        in_specs=[pl.BlockSpec((r, d), lambda l, j: (0, 0)),
                  pl.BlockSpec((1, d, tn), lambda l, j: (l, 0, j)),
                  pl.BlockSpec((1, 1, tn), lambda l, j: (l, 0, j))],
        out_specs=pl.BlockSpec((1, r, tn), lambda l, j: (l, 0, j)),
        out_shape=jax.ShapeDtypeStruct((depth, r, n), F32),
        compiler_params=_params("arbitrary", "arbitrary"),
        name="ada_table",
    )(cond, ada_w, ada_b.reshape(depth, 1, n))


def _mod_index(per_batch):
    return (lambda b, i: (b, 0, 0)) if per_batch else (lambda b, i: (0, 0, 0))


def _norm_mod(x, g_row, mod, shift_row, scale_row):
    h = _rms_rows(x) * g_row
    return h * (1.0 + mod[scale_row:scale_row + 1]) + mod[shift_row:shift_row + 1]


EVEN_NIN = 1408


def _even_prep_body(s_ref, mod_ref, g_ref, win_ref, wq_ref, wkv_ref, gqa_ref, gka_ref, gqb_ref, gkvb_ref,
                    bdq_ref, bdk_ref, ca_ref, sa_ref, cb_ref, sb_ref,
                    qa_ref, ka_ref, va_ref, qb_ref, kb_ref, vb_ref):
    h = _norm_mod(s_ref[0], g_ref[0:1], mod_ref[0], 0, 1).astype(BF16)
    p = _dot(h, win_ref[...])
    aq, bqa = p[:, 0:512], p[:, 512:768]
    ak, av, bkva = p[:, 768:896], p[:, 896:1024], p[:, 1024:1152]
    kr, krs = p[:, 1152:1280], p[:, 1280:1408]
    ca, sa, cb, sb = ca_ref[...], sa_ref[...], cb_ref[...], sb_ref[...]

    aqn = aq * lax.rsqrt(_dot((aq * aq).astype(BF16), bdq_ref[...]) + EPS) * gqa_ref[...]
    aqs = _swap_halves(aqn, A_HEAD_DIM // 2)
    for blk in range(4):
        sl = slice(blk * LANE, (blk + 1) * LANE)
        q2 = ((aqn[:, sl] * ca + aqs[:, sl] * sa) * A_SCALE).astype(BF16)
        qa_ref[0, 2 * blk] = q2[:, :A_HEAD_DIM]
        qa_ref[0, 2 * blk + 1] = q2[:, A_HEAD_DIM:]
    akn = ak * lax.rsqrt(_dot((ak * ak).astype(BF16), bdk_ref[...]) + EPS) * gka_ref[...]
    akr = (akn * ca + _swap_halves(akn, A_HEAD_DIM // 2) * sa).astype(BF16)
    avb = av.astype(BF16)
    for j in range(A_KV_HEADS):
        ka_ref[0, j] = akr[:, j * A_HEAD_DIM:(j + 1) * A_HEAD_DIM]
        va_ref[0, j] = avb[:, j * A_HEAD_DIM:(j + 1) * A_HEAD_DIM]

    bqn = (_rms_rows(bqa) * gqb_ref[...]).astype(BF16)
    nq = B_HEADS * B_PAD_DIM
    u = _dot(bqn, wq_ref[:, :nq])
    us = _dot(bqn, wq_ref[:, nq:])
    bkvn = (_rms_rows(bkva) * gkvb_ref[...]).astype(BF16)
    uk = _dot(bkvn, wkv_ref[:, :nq])
    uv = _dot(bkvn, wkv_ref[:, nq:]).astype(BF16)
    krr = kr * cb + krs * sb
    for hd in range(B_HEADS):
        sl = slice(hd * B_PAD_DIM, (hd + 1) * B_PAD_DIM)
        qb_ref[0, hd] = ((u[:, sl] * cb + us[:, sl] * sb) * B_SCALE).astype(BF16)
        kb_ref[0, hd] = (uk[:, sl] + krr).astype(BF16)
        vb_ref[0, hd] = uv[:, hd * B_V_DIM:(hd + 1) * B_V_DIM]


def _even_prep(s, mod, g, ew, rope, tm):
    bsz, t, d = s.shape
    tm = min(tm, t)
    per_batch = mod.shape[0] != 1
    row = lambda b, i: (b, i, 0)
    hrow = lambda b, i: (b, 0, i, 0)
    tab = lambda b, i: (i, 0)
    consts = [ew["win"], ew["wq"], ew["wkv"], ew["gqa"], ew["gka"], ew["gqb"], ew["gkvb"], ew["bdq"], ew["bdk"]]
    outs = [(A_HEADS, A_HEAD_DIM), (A_KV_HEADS, A_HEAD_DIM), (A_KV_HEADS, A_HEAD_DIM),
            (B_HEADS, B_PAD_DIM), (B_HEADS, B_PAD_DIM), (B_HEADS, B_V_DIM)]
    return pl.pallas_call(
        _even_prep_body,
        grid=(bsz, t // tm),
        in_specs=[pl.BlockSpec((1, tm, d), row),
                  pl.BlockSpec((1, 6, d), _mod_index(per_batch)),
                  _const_spec(g.shape)]
        + [_const_spec(c.shape) for c in consts]
        + [pl.BlockSpec((tm, LANE), tab)] * 4,
        out_specs=[pl.BlockSpec((1, nh, tm, hd), hrow) for nh, hd in outs],
        out_shape=[jax.ShapeDtypeStruct((bsz, nh, t, hd), BF16) for nh, hd in outs],
        compiler_params=_params("parallel", "arbitrary"),
        name="even_prep",
    )(s, mod, g, *consts, *rope)


def _softmax_attend(q, kvs):
    ss = [_dot_nt(q, k) for k, _ in kvs]
    m = ss[0].max(axis=-1, keepdims=True)
    for s in ss[1:]:
        m = jnp.maximum(m, s.max(axis=-1, keepdims=True))
    acc, l = None, None
    for s, (_, v) in zip(ss, kvs):
        p = jnp.exp(s - m)
        ls = p.sum(axis=-1, keepdims=True)
        o = _dot(p.astype(BF16), v)
        acc = o if acc is None else acc + o
        l = ls if l is None else l + ls
    return acc / l


def _attn_body(n_src, qa_ref, qb_ref, *refs):
    kv_refs = refs[:4 * n_src]
    oa_ref, ob_ref = refs[4 * n_src:]
    srcs = [kv_refs[4 * i:4 * i + 4] for i in range(n_src)]
    for pair in range(A_HEADS // 2):
        outs = []
        for hd in (2 * pair, 2 * pair + 1):
            j = hd // A_GROUP
            outs.append(_softmax_attend(qa_ref[0, hd], [(ka[0, j], va[0, j]) for ka, va, _, _ in srcs]))
        oa_ref[0, :, pair * LANE:(pair + 1) * LANE] = jnp.concatenate(outs, axis=-1).astype(BF16)
    for pair in range(B_HEADS // 2):
        outs = []
        for hd in (2 * pair, 2 * pair + 1):
            outs.append(_softmax_attend(qb_ref[0, hd], [(kb[0, hd], vb[0, hd]) for _, _, kb, vb in srcs]))
        ob_ref[0, :, pair * LANE:(pair + 1) * LANE] = jnp.concatenate(outs, axis=-1).astype(BF16)


def _attention(q, kv_srcs, tq):
    qa, qb = q
    bsz, _, t, _ = qa.shape
    tq = min(tq, t)
    qrow = lambda b, i: (b, 0, i, 0)
    whole = lambda b, i: (b, 0, 0, 0)
    in_specs = [pl.BlockSpec((1, A_HEADS, tq, A_HEAD_DIM), qrow), pl.BlockSpec((1, B_HEADS, tq, B_PAD_DIM), qrow)]
    args = [qa, qb]
    for src in kv_srcs:
        for a in src:
            in_specs.append(pl.BlockSpec((1,) + a.shape[1:], whole))
            args.append(a)
    out = jax.ShapeDtypeStruct((bsz, t, MIX_HALF), BF16)
    return pl.pallas_call(
        functools.partial(_attn_body, len(kv_srcs)),
        grid=(bsz, t // tq),
        in_specs=in_specs,
        out_specs=[pl.BlockSpec((1, tq, MIX_HALF), lambda b, i: (b, i, 0))] * 2,
        out_shape=[out, out],
        compiler_params=_params("parallel", "arbitrary"),
        name="attention",
    )(*args)


def _post_body(s_ref, o1_ref, o2_ref, mod_ref, g_ref, wo1_ref, wo2_ref, w1_ref, w2_ref, out_ref):
    mod = mod_ref[0]
    mo = _dot(o1_ref[0], wo1_ref[...]) + _dot(o2_ref[0], wo2_ref[...])
    x1 = s_ref[0] + mod[2:3] * (_rms_rows(mo) * g_ref[1:2])
    h = _norm_mod(x1, g_ref[2:3], mod, 3, 4).astype(BF16)
    acc = None
    for c in range(FF_HIDDEN // FF_CHUNK):
        sl = slice(c * FF_CHUNK, (c + 1) * FF_CHUNK)
        u = jnp.square(jnp.maximum(_dot(h, w1_ref[:, sl]), 0.0)).astype(BF16)
        part = _dot(u, w2_ref[sl, :])
        acc = part if acc is None else acc + part
    out_ref[0] = x1 + mod[5:6] * (_rms_rows(acc) * g_ref[3:4])


def _post(s, o1, o2, mod, g, wo1, wo2, w1, w2, tm):
    bsz, t, d = s.shape
    tm = min(tm, t)
    per_batch = mod.shape[0] != 1
    row = lambda b, i: (b, i, 0)
    return pl.pallas_call(
        _post_body,
        grid=(bsz, t // tm),
        in_specs=[pl.BlockSpec((1, tm, d), row),
                  pl.BlockSpec((1, tm, MIX_HALF), row),
                  pl.BlockSpec((1, tm, MIX_HALF), row),
                  pl.BlockSpec((1, 6, d), _mod_index(per_batch)),
                  _const_spec(g.shape), _const_spec(wo1.shape), _const_spec(wo2.shape),
                  _const_spec(w1.shape), _const_spec(w2.shape)],
        out_specs=pl.BlockSpec((1, tm, d), row),
        out_shape=jax.ShapeDtypeStruct(s.shape, F32),
        input_output_aliases={0: 0},
        compiler_params=_params("parallel", "arbitrary"),
        name="post",
    )(s, o1, o2, mod, g, wo1, wo2, w1, w2)


ODD_NIN = ODD_HY + ODD_Z + ODD_XBC + LANE


def _softplus(x):
    return jnp.maximum(x, 0.0) + jnp.log1p(jnp.exp(-jnp.abs(x)))


def _odd_prep_body(s_ref, mod_ref, g_ref, win_ref, dtb_ref, hy_ref, z_ref, xbc_ref, dt_ref):
    h = _norm_mod(s_ref[0], g_ref[0:1], mod_ref[0], 0, 1).astype(BF16)
    p = _dot(h, win_ref[...])
    hy_ref[0] = p[:, :ODD_HY].astype(BF16)
    z_ref[0] = p[:, ODD_HY:ODD_HY + ODD_Z].astype(BF16)
    xbc_ref[0] = p[:, ODD_HY + ODD_Z:ODD_HY + ODD_Z + ODD_XBC].astype(BF16)
    dt_ref[0] = _softplus(p[:, ODD_HY + ODD_Z + ODD_XBC:] + dtb_ref[...])


def _odd_prep(s, mod, g, win, dtb, tm):
    bsz, t, d = s.shape
    tm = min(tm, t)
    per_batch = mod.shape[0] != 1
    row = lambda b, i: (b, i, 0)
    widths = [(ODD_HY, BF16), (ODD_Z, BF16), (ODD_XBC, BF16), (LANE, F32)]
    return pl.pallas_call(
        _odd_prep_body,
        grid=(bsz, t // tm),
        in_specs=[pl.BlockSpec((1, tm, d), row),
                  pl.BlockSpec((1, 6, d), _mod_index(per_batch)),
                  _const_spec(g.shape), _const_spec(win.shape), _const_spec(dtb.shape)],
        out_specs=[pl.BlockSpec((1, tm, w), row) for w, _ in widths],
        out_shape=[jax.ShapeDtypeStruct((bsz, t, w), dt) for w, dt in widths],
        compiler_params=_params("parallel", "arbitrary"),
        name="odd_prep",
    )(s, mod, g, win, dtb)


def _filter_body(z_ref, w1_ref, b1_ref, w2_ref, b2_ref, w3_ref, fr_ref, dec_ref, o_ref):
    fr = fr_ref[...]
    h = jnp.sin(fr * (_dot_f32(z_ref[...], w1_ref[...]) + b1_ref[...]))
    h = jnp.sin(fr * (_dot_f32(h, w2_ref[...]) + b2_ref[...]))
    o_ref[...] = _dot_f32(h, w3_ref[...]) * dec_ref[...]


def _hyena_taps(n, w1, b1, w2, b2, w3, freq):
    t = jnp.linspace(0.0, 1.0, n, dtype=F32)
    bands = (C_POS_EMB - 1) // 2
    w = 2.0 * math.pi * jnp.arange(n, dtype=F32) / n
    fb = jnp.linspace(1e-4, bands - 1, bands, dtype=F32)
    ph = w[:, None] * fb[None, :]
    z = jnp.concatenate([t[:, None], jnp.cos(ph), -jnp.sin(ph)], axis=-1)
    decay_max = math.log(C_DECAY_TARGET) / C_DECAY_FRAC_SHORT
    decay_min = math.log(C_DECAY_TARGET) / C_DECAY_FRAC_LONG
    deltas = jnp.abs(jnp.linspace(decay_min, decay_max, C_WIDTH, dtype=F32))
    dec = jnp.tile(jnp.exp(-t[:, None] * deltas[None, :]), (1, 2 * C_ORDER))
    pad_h = LANE - C_FILTER_HIDDEN
    zp = jnp.pad(z, ((0, 0), (0, LANE - C_POS_EMB)))
    w1p = jnp.pad(w1, ((0, LANE - C_POS_EMB), (0, pad_h)))
    w2p = jnp.pad(w2, ((0, pad_h), (0, pad_h)))
    w3p = jnp.pad(w3, ((0, pad_h), (0, 0)))
    row = lambda v: jnp.pad(v, (0, pad_h)).reshape(1, LANE)
    nout = 2 * C_ORDER * C_WIDTH
    tn = 512
    tr = min(n, 512)
    return pl.pallas_call(
        _filter_body,
        grid=(n // tr, nout // tn),
        in_specs=[pl.BlockSpec((tr, LANE), lambda i, j: (i, 0)),
                  pl.BlockSpec((LANE, LANE), lambda i, j: (0, 0)),
                  pl.BlockSpec((1, LANE), lambda i, j: (0, 0)),
                  pl.BlockSpec((LANE, LANE), lambda i, j: (0, 0)),
                  pl.BlockSpec((1, LANE), lambda i, j: (0, 0)),
                  pl.BlockSpec((LANE, tn), lambda i, j: (0, j)),
                  pl.BlockSpec((1, LANE), lambda i, j: (0, 0)),
                  pl.BlockSpec((tr, tn), lambda i, j: (i, j))],
        out_specs=pl.BlockSpec((tr, tn), lambda i, j: (i, j)),
        out_shape=jax.ShapeDtypeStruct((n, nout), F32),
        compiler_params=_params("arbitrary", "arbitrary"),
        name="hyena_taps",
    )(zp, w1p, row(b1), w2p, row(b2), w3p, row(freq), dec)


def _dft_matrices(n):
    m = 2 * n
    k = jnp.arange(n, dtype=jnp.int32)[:, None]
    s = jnp.arange(n, dtype=jnp.int32)[None, :]
    ang = ((k * s) % m).astype(F32) * (2.0 * math.pi / m)
    nyq = jnp.where(s % 2 == 0, 1.0, -1.0).astype(F32)
    cosm = jnp.cos(ang)
    sinm = jnp.where(k == 0, nyq, -jnp.sin(ang))
    fwd = jnp.concatenate([cosm, sinm], axis=0)
    wgt = jnp.where(k == 0, 1.0 / m, 2.0 / m)
    inv = jnp.concatenate([(cosm * wgt).T, (sinm * wgt).T], axis=1)
    return fwd.astype(BF16), inv.astype(BF16)


DFT_ROWS = 256


def _first_row(shape, offset=0):
    return lax.broadcasted_iota(jnp.int32, shape, 0) + offset == 0


def _dft_row_tiles(n, step):
    tr = min(DFT_ROWS, n)

    def body(r, carry):
        r0 = pl.multiple_of(r * tr, tr)
        step(r0, pl.ds(r0, tr), pl.ds(pl.multiple_of(n + r0, tr), tr))
        return carry
    lax.fori_loop(0, n // tr, body, 0)


def _spectrum_body(n, f_ref, fw_ref, bw_ref, o_ref, fs_ref, bs_ref):
    fw = fw_ref[...]
    bw = jnp.where(_first_row(fw.shape), 0.0, bw_ref[...])
    for u, s_ref in ((fw, fs_ref), (bw, bs_ref)):
        hi = u.astype(BF16)
        s_ref[0] = hi
        s_ref[1] = (u - hi.astype(F32)).astype(BF16)

    def step(r0, rows, rows_im):
        def packed(s_ref, rr):
            return _dot(f_ref[rr, :], s_ref[0]) + _dot(f_ref[rr, :], s_ref[1])
        o_ref[0, rows, :] = packed(fs_ref, rows) + packed(bs_ref, rows)
        im1, im2 = packed(fs_ref, rows_im), packed(bs_ref, rows_im)
        o_ref[0, rows_im, :] = jnp.where(_first_row(im1.shape, r0), im1 + im2, im1 - im2)
    _dft_row_tiles(n, step)


def _hyena_spectrum(taps, fwd, cw):
    n = taps.shape[0]
    nb = C_WIDTH // cw
    return pl.pallas_call(
        functools.partial(_spectrum_body, n),
        grid=(C_ORDER, nb),
        in_specs=[_const_spec(fwd.shape),
                  pl.BlockSpec((n, cw), lambda o, j: (0, o * 2 * nb + j)),
                  pl.BlockSpec((n, cw), lambda o, j: (0, o * 2 * nb + nb + j))],
        out_specs=pl.BlockSpec((1, 2 * n, cw), lambda o, j: (o, 0, j)),
        out_shape=jax.ShapeDtypeStruct((C_ORDER, 2 * n, C_WIDTH), F32),
        scratch_shapes=[pltpu.VMEM((2, n, cw), BF16), pltpu.VMEM((2, n, cw), BF16)],
        compiler_params=_params("arbitrary", "arbitrary"),
        name="hyena_spectrum",
    )(fwd, taps, taps)


def _hy_fwd_body(n, do_conv, src_ref, cw_ref, cb_ref, f_ref, h_ref, o_ref, zb_ref):
    if do_conv:
        zb_ref[...] = _short_conv(src_ref[0].astype(F32), cw_ref, cb_ref).astype(BF16)
    else:
        zb_ref[...] = src_ref[0]

    def step(r0, rows, rows_im):
        re = _dot(f_ref[rows, :], zb_ref[...])
        im = _dot(f_ref[rows_im, :], zb_ref[...])
        hre, him = h_ref[0, rows, :], h_ref[0, rows_im, :]
        first = _first_row(re.shape, r0)
        imim = im * him
        o_ref[0, rows, :] = (re * hre - jnp.where(first, 0.0, imim)).astype(BF16)
        o_ref[0, rows_im, :] = jnp.where(first, imim, re * him + im * hre).astype(BF16)
    _dft_row_tiles(n, step)


def _hy_fwd(src, part, do_conv, conv_w, conv_b, fwd, spec, order, cw):
    bsz, n, _ = src.shape
    nb = C_WIDTH // cw
    return pl.pallas_call(
        functools.partial(_hy_fwd_body, n, do_conv),
        grid=(nb, bsz),
        in_specs=[pl.BlockSpec((1, n, cw), lambda j, b: (b, 0, part * nb + j)),
                  pl.BlockSpec((3, cw), lambda j, b: (0, part * nb + j)),
                  pl.BlockSpec((1, cw), lambda j, b: (0, part * nb + j)),
                  _const_spec(fwd.shape),
                  pl.BlockSpec((1, 2 * n, cw), lambda j, b: (order, 0, j))],
        out_specs=pl.BlockSpec((1, 2 * n, cw), lambda j, b: (b, 0, j)),
        out_shape=jax.ShapeDtypeStruct((bsz, 2 * n, C_WIDTH), BF16),
        scratch_shapes=[pltpu.VMEM((n, cw), BF16)],
        compiler_params=_params("parallel", "arbitrary"),
        name="hyena_fwd",
    )(src, conv_w, conv_b, fwd, spec)


def _hy_inv_body(n, prev_conv, yf_ref, g_ref, prev_ref, pw_ref, pb_ref, gate_ref, gw_ref, gb_ref, skip_ref, o_ref,
                 gate_s, skip_s):
    prev = prev_ref[0].astype(F32)
    if prev_conv:
        prev = _short_conv(prev, pw_ref, pb_ref)
    gate = _short_conv(gate_ref[0].astype(F32), gw_ref, gb_ref)
    gate_s[...] = gate
    skip_s[...] = gate * (prev * skip_ref[0])
    tr = min(DFT_ROWS, n)

    def body(r, carry):
        rows = pl.ds(pl.multiple_of(r * tr, tr), tr)
        y = _dot(g_ref[rows, :], yf_ref[0])
        o_ref[0, rows, :] = (gate_s[rows, :] * y + skip_s[rows, :]).astype(BF16)
        return carry
    lax.fori_loop(0, n // tr, body, 0)


def _hy_inv(yf, inv, prev, prev_part, prev_conv, hy, gate_part, conv_w, conv_b, skip, order, cw):
    bsz, n, _ = hy.shape
    nb = C_WIDTH // cw
    return pl.pallas_call(
        functools.partial(_hy_inv_body, n, prev_conv),
        grid=(nb, bsz),
        in_specs=[pl.BlockSpec((1, 2 * n, cw), lambda j, b: (b, 0, j)),
                  _const_spec(inv.shape),
                  pl.BlockSpec((1, n, cw), lambda j, b: (b, 0, prev_part * nb + j)),
                  pl.BlockSpec((3, cw), lambda j, b: (0, prev_part * nb + j)),
                  pl.BlockSpec((1, cw), lambda j, b: (0, prev_part * nb + j)),
                  pl.BlockSpec((1, n, cw), lambda j, b: (b, 0, gate_part * nb + j)),
                  pl.BlockSpec((3, cw), lambda j, b: (0, gate_part * nb + j)),
                  pl.BlockSpec((1, cw), lambda j, b: (0, gate_part * nb + j)),
                  pl.BlockSpec((1, 1, cw), lambda j, b: (order, 0, j))],
        out_specs=pl.BlockSpec((1, n, cw), lambda j, b: (b, 0, j)),
        out_shape=jax.ShapeDtypeStruct((bsz, n, C_WIDTH), BF16),
        scratch_shapes=[pltpu.VMEM((n, cw), F32), pltpu.VMEM((n, cw), F32)],
        compiler_params=_params("parallel", "arbitrary"),
        name="hyena_inv",
    )(yf, inv, prev, conv_w, conv_b, hy, conv_w, conv_b, skip)


def _hyena(hy, conv_w, conv_b, skip, filt):
    n = hy.shape[1]
    cw = 256
    taps = _hyena_taps(n, *filt)
    fwd, inv = _dft_matrices(n)
    spec = _hyena_spectrum(taps, fwd, cw)
    skip3 = skip.reshape(C_ORDER, 1, C_WIDTH)
    yf = _hy_fwd(hy, 0, True, conv_w, conv_b, fwd, spec, 0, cw)
    z1 = _hy_inv(yf, inv, hy, 0, True, hy, 1, conv_w, conv_b, skip3, 0, cw)
    yf = _hy_fwd(z1, 0, False, conv_w, conv_b, fwd, spec, 1, cw)
    return _hy_inv(yf, inv, z1, 0, False, hy, 2, conv_w, conv_b, skip3, 1, cw)


def _ssd_chunk(act_ref, dt_ref, y_ref, st_ref, start, d, a_row, want_y):
    n = D_CHUNK
    rows = pl.ds(start, n)
    xs = act_ref[rows, 0:D_INNER]
    bm = act_ref[rows, D_INNER:D_INNER + D_GROUPS * D_STATE]
    cm = act_ref[rows, D_INNER + D_GROUPS * D_STATE:]
    dtc = dt_ref[0, rows, :]
    adt = dtc * a_row
    r = lax.broadcasted_iota(jnp.int32, (n, n), 0)
    c = lax.broadcasted_iota(jnp.int32, (n, n), 1)
    causal = (r >= c) if d == 0 else (r <= c)
    acs = _dot_f32(causal.astype(F32), adt)
    acs_t = acs.T
    total = jnp.sum(adt, axis=0, keepdims=True)
    heads_per_group = D_HEADS // D_GROUPS
    gw = heads_per_group * D_HEAD_DIM
    ys = []
    for g in range(D_GROUPS):
        bg = bm[:, g * D_STATE:(g + 1) * D_STATE]
        cg = cm[:, g * D_STATE:(g + 1) * D_STATE].astype(BF16)
        bgt = bg.T.astype(BF16)
        st = st_ref[g]
        if want_y:
            gram = _dot_nt(cg, bg.astype(BF16))
            carried = _dot(cg, st.astype(BF16))
        new_cols = []
        for e in range(heads_per_group):
            hd = g * heads_per_group + e
            col = d * D_HEADS + hd
            a_col = acs[:, col:col + 1]
            xdt = xs[:, hd * D_HEAD_DIM:(hd + 1) * D_HEAD_DIM] * dtc[:, col:col + 1]
            tot = total[:, col:col + 1]
            if want_y:
                decay = jnp.where(causal, jnp.exp(a_col - acs_t[col:col + 1, :]), 0.0)
                y_in = _dot((gram * decay).astype(BF16), xdt.astype(BF16))
                ys.append(y_in + jnp.exp(a_col) * carried[:, e * D_HEAD_DIM:(e + 1) * D_HEAD_DIM])
            upd = _dot(bgt, (xdt * jnp.exp(tot - a_col)).astype(BF16))
            new_cols.append(jnp.exp(tot) * st[:, e * D_HEAD_DIM:(e + 1) * D_HEAD_DIM] + upd)
        st_ref[g] = jnp.concatenate(new_cols, axis=-1)
    if want_y:
        y = jnp.concatenate(ys, axis=-1)
        if d == 0:
            y_ref[rows, :] = y
        else:
            y_ref[rows, :] = y_ref[rows, :] + y


def _ssd_body(need_ctx, *refs):
    if need_ctx:
        (xc_ref, dtc_ref, zc_ref, xx_ref, dtx_ref, zx_ref, cw_ref, cb_ref, alog_ref, dskip_ref, mn_ref,
         oc_ref, ox_ref, actc_ref, actx_ref, yc_ref, yx_ref, st_ref) = refs
    else:
        (xc_ref, dtc_ref, xx_ref, dtx_ref, zx_ref, cw_ref, cb_ref, alog_ref, dskip_ref, mn_ref,
         ox_ref, actc_ref, actx_ref, yx_ref, st_ref) = refs
        zc_ref = oc_ref = yc_ref = None
    for src, act in ((xc_ref, actc_ref), (xx_ref, actx_ref)):
        for blk in range(ODD_XBC // LANE):
            sl = slice(blk * LANE, (blk + 1) * LANE)
            v = _short_conv(src[0, :, sl].astype(F32), cw_ref.at[:, sl], cb_ref.at[:, sl])
            act[:, sl] = _silu(v)
    a_row = -jnp.exp(alog_ref[...])
    nc_c = actc_ref.shape[0] // D_CHUNK
    nc_x = actx_ref.shape[0] // D_CHUNK
    for d in range(2):
        st_ref[...] = jnp.zeros_like(st_ref)

        def ctx_step(k, carry, d=d):
            kk = k if d == 0 else nc_c - 1 - k
            _ssd_chunk(actc_ref, dtc_ref, yc_ref, st_ref, pl.multiple_of(kk * D_CHUNK, D_CHUNK), d, a_row, need_ctx)
            return carry

        def lat_step(k, carry, d=d):
            kk = k if d == 0 else nc_x - 1 - k
            _ssd_chunk(actx_ref, dtx_ref, yx_ref, st_ref, pl.multiple_of(kk * D_CHUNK, D_CHUNK), d, a_row, True)
            return carry

        lax.fori_loop(0, nc_c, ctx_step, 0)
        lax.fori_loop(0, nc_x, lat_step, 0)

    def finish(act, y_ref, z_ref, o_ref):
        def step(k, carry):
            rows = pl.ds(pl.multiple_of(k * D_CHUNK, D_CHUNK), D_CHUNK)
            y = y_ref[rows, :] + act[rows, 0:D_INNER] * dskip_ref[...]
            yz = y * _silu(z_ref[0, rows, :].astype(F32))
            o_ref[0, rows, :] = (_rms_rows(yz) * mn_ref[...]).astype(BF16)
            return carry
        lax.fori_loop(0, act.shape[0] // D_CHUNK, step, 0)

    finish(actx_ref, yx_ref, zx_ref, ox_ref)
    if need_ctx:
        finish(actc_ref, yc_ref, zc_ref, oc_ref)


def _ssd(ctx_in, lat_in, conv_w, conv_b, alog, dskip, mnorm, need_ctx):
    xc, dtc, zc = ctx_in
    xx, dtx, zx = lat_in
    bsz, n, _ = xx.shape
    nc = xc.shape[1]
    blk = lambda a: pl.BlockSpec((1,) + a.shape[1:], lambda b: (b, 0, 0))
    seqs = [xc, dtc] + ([zc] if need_ctx else []) + [xx, dtx, zx]
    consts = [conv_w, conv_b, alog, dskip, mnorm]
    out_x = jax.ShapeDtypeStruct((bsz, n, D_INNER), BF16)
    out_c = jax.ShapeDtypeStruct((bsz, nc, D_INNER), BF16)
    out_shape = [out_c, out_x] if need_ctx else [out_x]
    scratch = [pltpu.VMEM((nc, ODD_XBC), F32), pltpu.VMEM((n, ODD_XBC), F32)]
    scratch += ([pltpu.VMEM((nc, D_INNER), F32)] if need_ctx else []) + [pltpu.VMEM((n, D_INNER), F32)]
    scratch += [pltpu.VMEM((D_GROUPS, D_STATE, D_INNER // D_GROUPS), F32)]
    outs = pl.pallas_call(
        functools.partial(_ssd_body, need_ctx),
        grid=(bsz,),
        in_specs=[blk(a) for a in seqs] + [_const_spec(c.shape) for c in consts],
        out_specs=[blk(o) for o in out_shape],
        out_shape=out_shape,
        scratch_shapes=scratch,
        compiler_params=_params("parallel"),
        name="ssd",
    )(*seqs, *consts)
    return (outs[0], outs[1]) if need_ctx else (None, outs[0])


def _axial_rope(n_tokens, rot_dim):
    rows = n_tokens // GRID_W
    r_idx, c_idx = jnp.meshgrid(jnp.arange(rows), jnp.arange(GRID_W), indexing="ij")
    quarter = rot_dim // 4
    inv_freq = ROPE_THETA ** (-jnp.arange(quarter, dtype=F32) / quarter)
    ang = jnp.concatenate([r_idx.reshape(-1, 1).astype(F32) * inv_freq,
                           c_idx.reshape(-1, 1).astype(F32) * inv_freq], axis=-1)
    return jnp.cos(ang), jnp.sin(ang)


def _rope_tables(n_tokens, rotate):
    ones, zeros = jnp.ones((n_tokens, 64), F32), jnp.zeros((n_tokens, 64), F32)
    if rotate:
        ca, sa = _axial_rope(n_tokens, A_HEAD_DIM)
        cb, sb = _axial_rope(n_tokens, B_ROPE_DIM)
    else:
        ca, sa = jnp.ones((n_tokens, 32), F32), jnp.zeros((n_tokens, 32), F32)
        cb, sb = jnp.ones((n_tokens, 16), F32), jnp.zeros((n_tokens, 16), F32)
    z32 = zeros[:, :32]
    return (jnp.tile(jnp.concatenate([ca, ca], -1), (1, 2)),
            jnp.tile(jnp.concatenate([-sa, sa], -1), (1, 2)),
            jnp.concatenate([ones, cb, cb, z32], -1),
            jnp.concatenate([zeros, -sb, sb, z32], -1))


def _even_weights(w_in, a_qn, a_kn, b_qn, b_kvn, b_wq, b_wkv):
    d = w_in.shape[0]
    ak, av, bkva, bkr, aq, bqa = jnp.split(w_in, [128, 256, 384, 416, 928], axis=1)
    z64, z32 = jnp.zeros((d, 64), F32), jnp.zeros((d, 32), F32)
    half = B_ROPE_DIM // 2
    kr = jnp.concatenate([z64, bkr, z32], 1)
    krs = jnp.concatenate([z64, bkr[:, half:], bkr[:, :half], z32], 1)
    win = jnp.concatenate([aq, bqa, ak, av, bkva, kr, krs], 1).astype(BF16)
    wq = b_wq.reshape(B_Q_RANK, B_HEADS, B_NOPE_DIM + B_ROPE_DIM)
    nope, r1, r2 = wq[..., :B_NOPE_DIM], wq[..., B_NOPE_DIM:B_NOPE_DIM + half], wq[..., B_NOPE_DIM + half:]
    zq64, zq32 = jnp.zeros_like(nope), jnp.zeros((B_Q_RANK, B_HEADS, 32), F32)
    wq1 = jnp.concatenate([nope, r1, r2, zq32], -1).reshape(B_Q_RANK, B_HEADS * B_PAD_DIM)
    wq2 = jnp.concatenate([zq64, r2, r1, zq32], -1).reshape(B_Q_RANK, B_HEADS * B_PAD_DIM)
    wkv = b_wkv.reshape(B_KV_RANK, B_HEADS, B_NOPE_DIM + B_V_DIM)
    wk = jnp.concatenate([wkv[..., :B_NOPE_DIM], jnp.zeros((B_KV_RANK, B_HEADS, 64), F32)], -1)
    wk = wk.reshape(B_KV_RANK, B_HEADS * B_PAD_DIM)
    wv = wkv[..., B_NOPE_DIM:].reshape(B_KV_RANK, B_HEADS * B_V_DIM)
    blockdiag = lambda n: (jnp.arange(n)[:, None] // A_HEAD_DIM == jnp.arange(n)[None, :] // A_HEAD_DIM)
    return dict(
        win=win,
        wq=jnp.concatenate([wq1, wq2], 1).astype(BF16),
        wkv=jnp.concatenate([wk, wv], 1).astype(BF16),
        gqa=jnp.tile(a_qn, A_HEADS).reshape(1, -1), gka=jnp.tile(a_kn, A_KV_HEADS).reshape(1, -1),
        gqb=b_qn.reshape(1, -1), gkvb=b_kvn.reshape(1, -1),
        bdq=(blockdiag(A_HEADS * A_HEAD_DIM) / A_HEAD_DIM).astype(BF16),
        bdk=(blockdiag(A_KV_HEADS * A_HEAD_DIM) / A_HEAD_DIM).astype(BF16))


def kernel(x, c, ctx, c_ctx, ada_w, ada_b, norm_g, ff_w1, ff_w2, at_w_in, at_w_out, ga_q_norm, ga_k_norm,
           ml_q_norm, ml_kv_norm, ml_w_q_up, ml_w_kv_up, lc_w_in, lc_w_out, hy_short_w, hy_short_b, hy_w1,
           hy_b1, hy_w2, hy_b2, hy_w3, hy_freq, hy_skip, mb_conv_w, mb_conv_b, mb_a_log, mb_dt_bias, mb_d_skip,
           mb_norm):
    bsz, n_lat, d = x.shape
    n_ctx = ctx.shape[1]
    depth = ada_w.shape[0]
    rows = -(-(bsz + 1) // 8) * 8
    cond = jnp.concatenate([c, c_ctx[None], jnp.zeros((rows - bsz - 1, d), F32)], axis=0)
    mods = _ada_table(cond, ada_w, ada_b)
    rope_x = _rope_tables(n_lat, True)
    rope_c = _rope_tables(n_ctx, False)
    tm = 512
    for i in range(depth):
        need_ctx = i < depth - 1
        j = i // 2
        g = norm_g[i]
        mod_x = mods[i, :bsz].reshape(bsz, 6, d)
        mod_c = mods[i, bsz:bsz + 1].reshape(1, 6, d)
        if i % 2 == 0:
            ew = _even_weights(at_w_in[j], ga_q_norm[j], ga_k_norm[j], ml_q_norm[j], ml_kv_norm[j],
                               ml_w_q_up[j], ml_w_kv_up[j])
            px = _even_prep(x, mod_x, g, ew, rope_x, tm)
            pc = _even_prep(ctx, mod_c, g, ew, rope_c, tm)
            o1x, o2x = _attention(px[0:4:3], [px[1:3] + px[4:6], pc[1:3] + pc[4:6]], 256)
            if need_ctx:
                o1c, o2c = _attention(pc[0:4:3], [pc[1:3] + pc[4:6]], 256)
            w_out = at_w_out[j].astype(BF16)
        else:
            pad = jnp.zeros((d, LANE - ODD_DT), F32)
            win = jnp.concatenate([lc_w_in[j], pad], axis=1).astype(BF16)
            dtb = jnp.pad(mb_dt_bias[j].reshape(-1), (0, LANE - ODD_DT)).reshape(1, LANE)
            hy_x, z_x, xbc_x, dt_x = _odd_prep(x, mod_x, g, win, dtb, tm)
            hy_c, z_c, xbc_c, dt_c = _odd_prep(ctx, mod_c, g, win, dtb, tm)
            filt = (hy_w1[j], hy_b1[j], hy_w2[j], hy_b2[j], hy_w3[j], hy_freq[j])
            sw, sb = hy_short_w[j], hy_short_b[j].reshape(1, -1)
            o1x = _hyena(hy_x, sw, sb, hy_skip[j], filt)
            alog = jnp.pad(mb_a_log[j].reshape(-1), (0, LANE - ODD_DT)).reshape(1, LANE)
            dskip = jnp.repeat(mb_d_skip[j], D_HEAD_DIM).reshape(1, D_INNER)
            o2c, o2x = _ssd((xbc_c, dt_c, z_c), (xbc_x, dt_x, z_x), mb_conv_w[j], mb_conv_b[j].reshape(1, -1),
                            alog, dskip, mb_norm[j].reshape(1, -1), need_ctx)
            if need_ctx:
                o1c = _hyena(hy_c, sw, sb, hy_skip[j], filt)
            w_out = lc_w_out[j].astype(BF16)
        w1, w2 = ff_w1[i].astype(BF16), ff_w2[i].astype(BF16)
        x = _post(x, o1x, o2x, mod_x, g, w_out[:MIX_HALF], w_out[MIX_HALF:], w1, w2, tm)
        if need_ctx:
            ctx = _post(ctx, o1c, o2c, mod_c, g, w_out[:MIX_HALF], w_out[MIX_HALF:], w1, w2, tm)
    return x
```

```python
import functools
import math

import jax
import jax.numpy as jnp
from jax import lax
from jax.experimental import pallas as pl
from jax.experimental.pallas import tpu as pltpu

F32 = jnp.float32
BF16 = jnp.bfloat16

D_MODEL = 1024
GRID_W = 64
EPS = 1e-6
ROPE_THETA = 10000.0
MIX_HALF = D_MODEL // 2
A_HEAD_DIM = 64
A_HEADS = MIX_HALF // A_HEAD_DIM
A_KV_HEADS = A_HEADS // 4
A_GROUP = A_HEADS // A_KV_HEADS
B_NOPE_DIM = 64
B_ROPE_DIM = 32
B_V_DIM = 64
B_HEADS = MIX_HALF // B_V_DIM
B_Q_RANK = D_MODEL // 4
B_KV_RANK = D_MODEL // 8
B_PAD_DIM = 128
C_WIDTH = MIX_HALF
C_ORDER = 2
C_POS_EMB = 33
C_FILTER_HIDDEN = 64
C_DECAY_TARGET = 1e-2
C_DECAY_FRAC_SHORT = 0.3
C_DECAY_FRAC_LONG = 1.5
D_INNER = MIX_HALF
D_HEAD_DIM = 64
D_HEADS = D_INNER // D_HEAD_DIM
D_GROUPS = 2
D_STATE = 128
D_CHUNK = 128
FF_HIDDEN = 4 * D_MODEL
FF_CHUNK = 1024
A_SCALE = A_HEAD_DIM ** -0.5
B_SCALE = (B_NOPE_DIM + B_ROPE_DIM) ** -0.5
LOG2E = math.log2(math.e)
V_DIM = 64
ODD_HY = (C_ORDER + 1) * C_WIDTH
ODD_Z = D_INNER
ODD_XBC = D_INNER + 2 * D_GROUPS * D_STATE
ODD_DT = 2 * D_HEADS
LANE = 128
VMEM_LIMIT = 56 * 2 ** 20


def _params(*sem):
    return pltpu.CompilerParams(dimension_semantics=sem, vmem_limit_bytes=VMEM_LIMIT)


def _const_spec(shape):
    nd = len(shape)
    return pl.BlockSpec(shape, lambda *_: (0,) * nd, pipeline_mode=pl.Buffered(1))


def _dot(a, b):
    return jnp.dot(a, b, preferred_element_type=F32)


def _dot_nt(a, b):
    return lax.dot_general(a, b, (((1,), (1,)), ((), ())), preferred_element_type=F32)


def _dot_f32(a, b):
    return jnp.dot(a, b, preferred_element_type=F32, precision=lax.Precision.HIGHEST)


def _rms_rows(x):
    return x * lax.rsqrt(jnp.mean(x * x, axis=-1, keepdims=True) + EPS)


def _silu(x):
    return x * jax.nn.sigmoid(x)


def _swap_halves(x, half):
    n = x.shape[-1]
    lane = lax.broadcasted_iota(jnp.int32, x.shape, 1)
    r1 = pltpu.roll(x, half, 1)
    r2 = pltpu.roll(x, n - half, 1)
    i1 = pltpu.roll(lane, half, 1)
    return jnp.where(i1 == (lane ^ half), r1, r2)


def _row_neighbours(u):
    n = u.shape[0]
    row = lax.broadcasted_iota(jnp.int32, u.shape, 0)
    prev = jnp.where(row == 0, 0.0, pltpu.roll(u, 1, 0))
    nxt = jnp.where(row == n - 1, 0.0, pltpu.roll(u, n - 1, 0))
    return prev, nxt


def _short_conv(u, w_ref, b_ref):
    prev, nxt = _row_neighbours(u)
    return w_ref[0:1, :] * prev + w_ref[1:2, :] * u + w_ref[2:3, :] * nxt + b_ref[...]


def _ada_body(cond_ref, w_ref, b_ref, o_ref):
    h = _silu(cond_ref[...]).astype(BF16)
    o_ref[0] = _dot(h, w_ref[0].astype(BF16)) + b_ref[0]


def _ada_table(cond, ada_w, ada_b):
    depth, d, n = ada_w.shape
    r = cond.shape[0]
    tn = 1536
    return pl.pallas_call(
        _ada_body,
        grid=(depth, n // tn),
        in_specs=[pl.BlockSpec((r, d), lambda l, j: (0, 0)),
                  pl.BlockSpec((1, d, tn), lambda l, j: (l, 0, j)),
                  pl.BlockSpec((1, 1, tn), lambda l, j: (l, 0, j))],
        out_specs=pl.BlockSpec((1, r, tn), lambda l, j: (l, 0, j)),
        out_shape=jax.ShapeDtypeStruct((depth, r, n), F32),
        compiler_params=_params("arbitrary", "arbitrary"),
        name="ada_table",
    )(cond, ada_w, ada_b.reshape(depth, 1, n))


def _mod_index(per_batch):
    return (lambda b, i: (b, 0, 0)) if per_batch else (lambda b, i: (0, 0, 0))


def _norm_mod(x, g_row, mod, shift_row, scale_row):
    h = _rms_rows(x) * g_row
    return h * (1.0 + mod[scale_row:scale_row + 1]) + mod[shift_row:shift_row + 1]


def _even_prep_body(s_ref, mod_ref, g_ref, win_ref, wq_ref, wkv_ref, gqa_ref, gka_ref, gqb_ref, gkvb_ref,
                    bdq_ref, bdk_ref, ca_ref, sa_ref, cb_ref, sb_ref,
                    qa_ref, ka_ref, va_ref, qb_ref, kb_ref, vb_ref):
    h = _norm_mod(s_ref[0], g_ref[0:1], mod_ref[0], 0, 1).astype(BF16)
    p = _dot(h, win_ref[...])
    aq, bqa = p[:, 0:512], p[:, 512:768]
    ak, av, bkva = p[:, 768:896], p[:, 896:1152], p[:, 1152:1280]
    kr, krs = p[:, 1280:1408], p[:, 1408:1536]
    ca, sa, cb, sb = ca_ref[...], sa_ref[...], cb_ref[...], sb_ref[...]
    upper = lax.broadcasted_iota(jnp.int32, (h.shape[0], LANE), 1) >= V_DIM

    aqn = aq * lax.rsqrt(_dot((aq * aq).astype(BF16), bdq_ref[...]) + EPS) * gqa_ref[...]
    aqs = _swap_halves(aqn, A_HEAD_DIM // 2)
    for blk in range(4):
        sl = slice(blk * LANE, (blk + 1) * LANE)
        q2 = ((aqn[:, sl] * ca + aqs[:, sl] * sa) * (A_SCALE * LOG2E)).astype(BF16)
        qa_ref[0, 2 * blk] = q2[:, :A_HEAD_DIM]
        qa_ref[0, 2 * blk + 1] = q2[:, A_HEAD_DIM:]
    akn = ak * lax.rsqrt(_dot((ak * ak).astype(BF16), bdk_ref[...]) + EPS) * gka_ref[...]
    akr = (akn * ca + _swap_halves(akn, A_HEAD_DIM // 2) * sa).astype(BF16)
    for j in range(A_KV_HEADS):
        ka_ref[0, j] = akr[:, j * A_HEAD_DIM:(j + 1) * A_HEAD_DIM]
        va_ref[0, j] = jnp.where(upper, 1.0, av[:, j * LANE:(j + 1) * LANE]).astype(BF16)

    bqn = (_rms_rows(bqa) * gqb_ref[...]).astype(BF16)
    nq = B_HEADS * B_PAD_DIM
    u = _dot(bqn, wq_ref[:, :nq])
    us = _dot(bqn, wq_ref[:, nq:])
    bkvn = (_rms_rows(bkva) * gkvb_ref[...]).astype(BF16)
    uk = _dot(bkvn, wkv_ref[:, :nq])
    uv = _dot(bkvn, wkv_ref[:, nq:])
    krr = kr * cb + krs * sb
    for hd in range(B_HEADS):
        sl = slice(hd * B_PAD_DIM, (hd + 1) * B_PAD_DIM)
        qb_ref[0, hd] = ((u[:, sl] * cb + us[:, sl] * sb) * (B_SCALE * LOG2E)).astype(BF16)
        kb_ref[0, hd] = (uk[:, sl] + krr).astype(BF16)
        vb_ref[0, hd] = jnp.where(upper, 1.0, uv[:, sl]).astype(BF16)


def _even_prep(s, mod, g, ew, rope, tm):
    bsz, t, d = s.shape
    tm = min(tm, t)
    per_batch = mod.shape[0] != 1
    row = lambda b, i: (b, i, 0)
    hrow = lambda b, i: (b, 0, i, 0)
    tab = lambda b, i: (i, 0)
    consts = [ew["win"], ew["wq"], ew["wkv"], ew["gqa"], ew["gka"], ew["gqb"], ew["gkvb"], ew["bdq"], ew["bdk"]]
    outs = [(A_HEADS, A_HEAD_DIM), (A_KV_HEADS, A_HEAD_DIM), (A_KV_HEADS, LANE),
            (B_HEADS, B_PAD_DIM), (B_HEADS, B_PAD_DIM), (B_HEADS, LANE)]
    return pl.pallas_call(
        _even_prep_body,
        grid=(bsz, t // tm),
        in_specs=[pl.BlockSpec((1, tm, d), row),
                  pl.BlockSpec((1, 6, d), _mod_index(per_batch)),
                  _const_spec(g.shape)]
        + [_const_spec(c.shape) for c in consts]
        + [pl.BlockSpec((tm, LANE), tab)] * 4,
        out_specs=[pl.BlockSpec((1, nh, tm, hd), hrow) for nh, hd in outs],
        out_shape=[jax.ShapeDtypeStruct((bsz, nh, t, hd), BF16) for nh, hd in outs],
        compiler_params=_params("parallel", "arbitrary"),
        name="even_prep",
    )(s, mod, g, *consts, *rope)


def _softmax_attend(q, kvs):
    ss = [_dot_nt(q, k) for k, _ in kvs]
    m = ss[0].max(axis=-1, keepdims=True)
    for s in ss[1:]:
        m = jnp.maximum(m, s.max(axis=-1, keepdims=True))
    acc = None
    for s, (_, v) in zip(ss, kvs):
        o = _dot(jnp.exp2(s - m).astype(BF16), v)
        acc = o if acc is None else acc + o
    return acc / pltpu.roll(acc, V_DIM, 1)


def _attn_body(n_src, qa_ref, qb_ref, *refs):
    kv_refs = refs[:4 * n_src]
    oa_ref, ob_ref = refs[4 * n_src:]
    srcs = [kv_refs[4 * i:4 * i + 4] for i in range(n_src)]
    lower = lax.broadcasted_iota(jnp.int32, (qa_ref.shape[2], LANE), 1) < V_DIM

    def pair_out(first, second):
        return jnp.where(lower, first, pltpu.roll(second, V_DIM, 1)).astype(BF16)

    for pair in range(A_HEADS // 2):
        outs = []
        for hd in (2 * pair, 2 * pair + 1):
            j = hd // A_GROUP
            outs.append(_softmax_attend(qa_ref[0, hd], [(ka[0, j], va[0, j]) for ka, va, _, _ in srcs]))
        oa_ref[0, :, pair * LANE:(pair + 1) * LANE] = pair_out(*outs)
    for pair in range(B_HEADS // 2):
        outs = []
        for hd in (2 * pair, 2 * pair + 1):
            outs.append(_softmax_attend(qb_ref[0, hd], [(kb[0, hd], vb[0, hd]) for _, _, kb, vb in srcs]))
        ob_ref[0, :, pair * LANE:(pair + 1) * LANE] = pair_out(*outs)


def _attention(q, kv_srcs, tq):
    qa, qb = q
    bsz, _, t, _ = qa.shape
    tq = min(tq, t)
    qrow = lambda b, i: (b, 0, i, 0)
    whole = lambda b, i: (b, 0, 0, 0)
    in_specs = [pl.BlockSpec((1, A_HEADS, tq, A_HEAD_DIM), qrow), pl.BlockSpec((1, B_HEADS, tq, B_PAD_DIM), qrow)]
    args = [qa, qb]
    for src in kv_srcs:
        for a in src:
            in_specs.append(pl.BlockSpec((1,) + a.shape[1:], whole))
            args.append(a)
    out = jax.ShapeDtypeStruct((bsz, t, MIX_HALF), BF16)
    return pl.pallas_call(
        functools.partial(_attn_body, len(kv_srcs)),
        grid=(bsz, t // tq),
        in_specs=in_specs,
        out_specs=[pl.BlockSpec((1, tq, MIX_HALF), lambda b, i: (b, i, 0))] * 2,
        out_shape=[out, out],
        compiler_params=_params("parallel", "arbitrary"),
        name="attention",
    )(*args)


def _post_body(s_ref, o1_ref, o2_ref, mod_ref, g_ref, wo1_ref, wo2_ref, w1_ref, w2_ref, out_ref):
    mod = mod_ref[0]
    mo = _dot(o1_ref[0], wo1_ref[...]) + _dot(o2_ref[0], wo2_ref[...])
    x1 = s_ref[0] + mod[2:3] * (_rms_rows(mo) * g_ref[1:2])
    h = _norm_mod(x1, g_ref[2:3], mod, 3, 4).astype(BF16)
    acc = None
    for c in range(FF_HIDDEN // FF_CHUNK):
        sl = slice(c * FF_CHUNK, (c + 1) * FF_CHUNK)
        u = jnp.square(jnp.maximum(_dot(h, w1_ref[:, sl]), 0.0)).astype(BF16)
        part = _dot(u, w2_ref[sl, :])
        acc = part if acc is None else acc + part
    out_ref[0] = x1 + mod[5:6] * (_rms_rows(acc) * g_ref[3:4])


def _post(s, o1, o2, mod, g, wo1, wo2, w1, w2, tm):
    bsz, t, d = s.shape
    tm = min(tm, t)
    per_batch = mod.shape[0] != 1
    row = lambda b, i: (b, i, 0)
    return pl.pallas_call(
        _post_body,
        grid=(bsz, t // tm),
        in_specs=[pl.BlockSpec((1, tm, d), row),
                  pl.BlockSpec((1, tm, MIX_HALF), row),
                  pl.BlockSpec((1, tm, MIX_HALF), row),
                  pl.BlockSpec((1, 6, d), _mod_index(per_batch)),
                  _const_spec(g.shape), _const_spec(wo1.shape), _const_spec(wo2.shape),
                  _const_spec(w1.shape), _const_spec(w2.shape)],
        out_specs=pl.BlockSpec((1, tm, d), row),
        out_shape=jax.ShapeDtypeStruct(s.shape, F32),
        input_output_aliases={0: 0},
        compiler_params=_params("parallel", "arbitrary"),
        name="post",
    )(s, o1, o2, mod, g, wo1, wo2, w1, w2)


ODD_NIN = ODD_HY + ODD_Z + ODD_XBC + LANE


def _softplus(x):
    return jnp.maximum(x, 0.0) + jnp.log1p(jnp.exp(-jnp.abs(x)))


def _odd_prep_body(s_ref, mod_ref, g_ref, win_ref, dtb_ref, hy_ref, z_ref, xbc_ref, dt_ref):
    h = _norm_mod(s_ref[0], g_ref[0:1], mod_ref[0], 0, 1).astype(BF16)
    p = _dot(h, win_ref[...])
    hy_ref[0] = p[:, :ODD_HY].astype(BF16)
    z_ref[0] = p[:, ODD_HY:ODD_HY + ODD_Z].astype(BF16)
    xbc_ref[0] = p[:, ODD_HY + ODD_Z:ODD_HY + ODD_Z + ODD_XBC].astype(BF16)
    dt_ref[0] = _softplus(p[:, ODD_HY + ODD_Z + ODD_XBC:] + dtb_ref[...])


def _odd_prep(s, mod, g, win, dtb, tm):
    bsz, t, d = s.shape
    tm = min(tm, t)
    per_batch = mod.shape[0] != 1
    row = lambda b, i: (b, i, 0)
    widths = [(ODD_HY, BF16), (ODD_Z, BF16), (ODD_XBC, BF16), (LANE, F32)]
    return pl.pallas_call(
        _odd_prep_body,
        grid=(bsz, t // tm),
        in_specs=[pl.BlockSpec((1, tm, d), row),
                  pl.BlockSpec((1, 6, d), _mod_index(per_batch)),
                  _const_spec(g.shape), _const_spec(win.shape), _const_spec(dtb.shape)],
        out_specs=[pl.BlockSpec((1, tm, w), row) for w, _ in widths],
        out_shape=[jax.ShapeDtypeStruct((bsz, t, w), dt) for w, dt in widths],
        compiler_params=_params("parallel", "arbitrary"),
        name="odd_prep",
    )(s, mod, g, win, dtb)


def _filter_body(z_ref, w1_ref, b1_ref, w2_ref, b2_ref, w3_ref, fr_ref, dec_ref, o_ref):
    fr = fr_ref[...]
    h = jnp.sin(fr * (_dot_f32(z_ref[...], w1_ref[...]) + b1_ref[...]))
    h = jnp.sin(fr * (_dot_f32(h, w2_ref[...]) + b2_ref[...]))
    o_ref[...] = _dot_f32(h, w3_ref[...]) * dec_ref[...]


def _hyena_taps(n, w1, b1, w2, b2, w3, freq):
    t = jnp.linspace(0.0, 1.0, n, dtype=F32)
    bands = (C_POS_EMB - 1) // 2
    w = 2.0 * math.pi * jnp.arange(n, dtype=F32) / n
    fb = jnp.linspace(1e-4, bands - 1, bands, dtype=F32)
    ph = w[:, None] * fb[None, :]
    z = jnp.concatenate([t[:, None], jnp.cos(ph), -jnp.sin(ph)], axis=-1)
    decay_max = math.log(C_DECAY_TARGET) / C_DECAY_FRAC_SHORT
    decay_min = math.log(C_DECAY_TARGET) / C_DECAY_FRAC_LONG
    deltas = jnp.abs(jnp.linspace(decay_min, decay_max, C_WIDTH, dtype=F32))
    dec = jnp.tile(jnp.exp(-t[:, None] * deltas[None, :]), (1, 2 * C_ORDER))
    pad_h = LANE - C_FILTER_HIDDEN
    zp = jnp.pad(z, ((0, 0), (0, LANE - C_POS_EMB)))
    w1p = jnp.pad(w1, ((0, LANE - C_POS_EMB), (0, pad_h)))
    w2p = jnp.pad(w2, ((0, pad_h), (0, pad_h)))
    w3p = jnp.pad(w3, ((0, pad_h), (0, 0)))
    row = lambda v: jnp.pad(v, (0, pad_h)).reshape(1, LANE)
    nout = 2 * C_ORDER * C_WIDTH
    tn = 512
    tr = min(n, 512)
    return pl.pallas_call(
        _filter_body,
        grid=(n // tr, nout // tn),
        in_specs=[pl.BlockSpec((tr, LANE), lambda i, j: (i, 0)),
                  pl.BlockSpec((LANE, LANE), lambda i, j: (0, 0)),
                  pl.BlockSpec((1, LANE), lambda i, j: (0, 0)),
                  pl.BlockSpec((LANE, LANE), lambda i, j: (0, 0)),
                  pl.BlockSpec((1, LANE), lambda i, j: (0, 0)),
                  pl.BlockSpec((LANE, tn), lambda i, j: (0, j)),
                  pl.BlockSpec((1, LANE), lambda i, j: (0, 0)),
                  pl.BlockSpec((tr, tn), lambda i, j: (i, j))],
        out_specs=pl.BlockSpec((tr, tn), lambda i, j: (i, j)),
        out_shape=jax.ShapeDtypeStruct((n, nout), F32),
        compiler_params=_params("arbitrary", "arbitrary"),
        name="hyena_taps",
    )(zp, w1p, row(b1), w2p, row(b2), w3p, row(freq), dec)


def _dft_matrices(n):
    m = 2 * n
    k = jnp.arange(n, dtype=jnp.int32)[:, None]
    s = jnp.arange(n, dtype=jnp.int32)[None, :]
    ang = ((k * s) % m).astype(F32) * (2.0 * math.pi / m)
    nyq = jnp.where(s % 2 == 0, 1.0, -1.0).astype(F32)
    cosm = jnp.cos(ang)
    sinm = jnp.where(k == 0, nyq, -jnp.sin(ang))
    fwd = jnp.concatenate([cosm, sinm], axis=0)
    wgt = jnp.where(k == 0, 1.0 / m, 2.0 / m)
    inv = jnp.concatenate([(cosm * wgt).T, (sinm * wgt).T], axis=1)
    return fwd.astype(BF16), inv.astype(BF16)


DFT_ROWS = 256


def _first_row(shape, offset=0):
    return lax.broadcasted_iota(jnp.int32, shape, 0) + offset == 0


def _dft_row_tiles(n, step):
    tr = min(DFT_ROWS, n)

    def body(r, carry):
        r0 = pl.multiple_of(r * tr, tr)
        step(r0, pl.ds(r0, tr), pl.ds(pl.multiple_of(n + r0, tr), tr))
        return carry
    lax.fori_loop(0, n // tr, body, 0)


def _spectrum_body(n, f_ref, fw_ref, bw_ref, o_ref, fs_ref, bs_ref):
    fw = fw_ref[...]
    bw = jnp.where(_first_row(fw.shape), 0.0, bw_ref[...])
    for u, s_ref in ((fw, fs_ref), (bw, bs_ref)):
        hi = u.astype(BF16)
        s_ref[0] = hi
        s_ref[1] = (u - hi.astype(F32)).astype(BF16)

    def step(r0, rows, rows_im):
        def packed(s_ref, rr):
            return _dot(f_ref[rr, :], s_ref[0]) + _dot(f_ref[rr, :], s_ref[1])
        o_ref[0, rows, :] = packed(fs_ref, rows) + packed(bs_ref, rows)
        im1, im2 = packed(fs_ref, rows_im), packed(bs_ref, rows_im)
        o_ref[0, rows_im, :] = jnp.where(_first_row(im1.shape, r0), im1 + im2, im1 - im2)
    _dft_row_tiles(n, step)


def _hyena_spectrum(taps, fwd, cw):
    n = taps.shape[0]
    nb = C_WIDTH // cw
    return pl.pallas_call(
        functools.partial(_spectrum_body, n),
        grid=(C_ORDER, nb),
        in_specs=[_const_spec(fwd.shape),
                  pl.BlockSpec((n, cw), lambda o, j: (0, o * 2 * nb + j)),
                  pl.BlockSpec((n, cw), lambda o, j: (0, o * 2 * nb + nb + j))],
        out_specs=pl.BlockSpec((1, 2 * n, cw), lambda o, j: (o, 0, j)),
        out_shape=jax.ShapeDtypeStruct((C_ORDER, 2 * n, C_WIDTH), F32),
        scratch_shapes=[pltpu.VMEM((2, n, cw), BF16), pltpu.VMEM((2, n, cw), BF16)],
        compiler_params=_params("arbitrary", "arbitrary"),
        name="hyena_spectrum",
    )(fwd, taps, taps)


def _hy_fwd_body(n, do_conv, src_ref, cw_ref, cb_ref, f_ref, h_ref, o_ref, zb_ref):
    if do_conv:
        zb_ref[...] = _short_conv(src_ref[0].astype(F32), cw_ref, cb_ref).astype(BF16)
    else:
        zb_ref[...] = src_ref[0]

    def step(r0, rows, rows_im):
        re = _dot(f_ref[rows, :], zb_ref[...])
        im = _dot(f_ref[rows_im, :], zb_ref[...])
        hre, him = h_ref[0, rows, :], h_ref[0, rows_im, :]
        first = _first_row(re.shape, r0)
        imim = im * him
        o_ref[0, rows, :] = (re * hre - jnp.where(first, 0.0, imim)).astype(BF16)
        o_ref[0, rows_im, :] = jnp.where(first, imim, re * him + im * hre).astype(BF16)
    _dft_row_tiles(n, step)


def _hy_fwd(src, part, do_conv, conv_w, conv_b, fwd, spec, order, cw):
    bsz, n, _ = src.shape
    nb = C_WIDTH // cw
    return pl.pallas_call(
        functools.partial(_hy_fwd_body, n, do_conv),
        grid=(nb, bsz),
        in_specs=[pl.BlockSpec((1, n, cw), lambda j, b: (b, 0, part * nb + j)),
                  pl.BlockSpec((3, cw), lambda j, b: (0, part * nb + j)),
                  pl.BlockSpec((1, cw), lambda j, b: (0, part * nb + j)),
                  _const_spec(fwd.shape),
                  pl.BlockSpec((1, 2 * n, cw), lambda j, b: (order, 0, j))],
        out_specs=pl.BlockSpec((1, 2 * n, cw), lambda j, b: (b, 0, j)),
        out_shape=jax.ShapeDtypeStruct((bsz, 2 * n, C_WIDTH), BF16),
        scratch_shapes=[pltpu.VMEM((n, cw), BF16)],
        compiler_params=_params("parallel", "arbitrary"),
        name="hyena_fwd",
    )(src, conv_w, conv_b, fwd, spec)


def _hy_inv_body(n, prev_conv, yf_ref, g_ref, prev_ref, pw_ref, pb_ref, gate_ref, gw_ref, gb_ref, skip_ref, o_ref,
                 gate_s, skip_s):
    prev = prev_ref[0].astype(F32)
    if prev_conv:
        prev = _short_conv(prev, pw_ref, pb_ref)
    gate = _short_conv(gate_ref[0].astype(F32), gw_ref, gb_ref)
    gate_s[...] = gate
    skip_s[...] = gate * (prev * skip_ref[0])
    tr = min(DFT_ROWS, n)

    def body(r, carry):
        rows = pl.ds(pl.multiple_of(r * tr, tr), tr)
        y = _dot(g_ref[rows, :], yf_ref[0])
        o_ref[0, rows, :] = (gate_s[rows, :] * y + skip_s[rows, :]).astype(BF16)
        return carry
    lax.fori_loop(0, n // tr, body, 0)


def _hy_inv(yf, inv, prev, prev_part, prev_conv, hy, gate_part, conv_w, conv_b, skip, order, cw):
    bsz, n, _ = hy.shape
    nb = C_WIDTH // cw
    return pl.pallas_call(
        functools.partial(_hy_inv_body, n, prev_conv),
        grid=(nb, bsz),
        in_specs=[pl.BlockSpec((1, 2 * n, cw), lambda j, b: (b, 0, j)),
                  _const_spec(inv.shape),
                  pl.BlockSpec((1, n, cw), lambda j, b: (b, 0, prev_part * nb + j)),
                  pl.BlockSpec((3, cw), lambda j, b: (0, prev_part * nb + j)),
                  pl.BlockSpec((1, cw), lambda j, b: (0, prev_part * nb + j)),
                  pl.BlockSpec((1, n, cw), lambda j, b: (b, 0, gate_part * nb + j)),
                  pl.BlockSpec((3, cw), lambda j, b: (0, gate_part * nb + j)),
                  pl.BlockSpec((1, cw), lambda j, b: (0, gate_part * nb + j)),
                  pl.BlockSpec((1, 1, cw), lambda j, b: (order, 0, j))],
        out_specs=pl.BlockSpec((1, n, cw), lambda j, b: (b, 0, j)),
        out_shape=jax.ShapeDtypeStruct((bsz, n, C_WIDTH), BF16),
        scratch_shapes=[pltpu.VMEM((n, cw), F32), pltpu.VMEM((n, cw), F32)],
        compiler_params=_params("parallel", "arbitrary"),
        name="hyena_inv",
    )(yf, inv, prev, conv_w, conv_b, hy, conv_w, conv_b, skip)


def _hyena(hy, conv_w, conv_b, skip, filt):
    n = hy.shape[1]
    cw = 256
    taps = _hyena_taps(n, *filt)
    fwd, inv = _dft_matrices(n)
    spec = _hyena_spectrum(taps, fwd, cw)
    skip3 = skip.reshape(C_ORDER, 1, C_WIDTH)
    yf = _hy_fwd(hy, 0, True, conv_w, conv_b, fwd, spec, 0, cw)
    z1 = _hy_inv(yf, inv, hy, 0, True, hy, 1, conv_w, conv_b, skip3, 0, cw)
    yf = _hy_fwd(z1, 0, False, conv_w, conv_b, fwd, spec, 1, cw)
    return _hy_inv(yf, inv, z1, 0, False, hy, 2, conv_w, conv_b, skip3, 1, cw)


def _split2(x):
    hi = x.astype(BF16)
    return hi, (x - hi.astype(F32)).astype(BF16)


def _select_cols(x, sel2):
    return _dot(jnp.concatenate(_split2(x), axis=1), sel2)


def _chunk_cumsum(adt, tri):
    hi = adt.astype(BF16)
    r1 = adt - hi.astype(F32)
    mid = r1.astype(BF16)
    lo = (r1 - mid.astype(F32)).astype(BF16)
    s = _dot(tri, jnp.concatenate([hi, mid, lo], axis=1))
    return s[:, :LANE] + s[:, LANE:2 * LANE] + s[:, 2 * LANE:]


def _ssd_chunk(act_ref, dt_ref, y_ref, st_ref, ehead_ref, erep_ref, start, d, a_row, want_y):
    n = D_CHUNK
    rows = pl.ds(start, n)
    dtc = dt_ref[0, rows, :]
    r = lax.broadcasted_iota(jnp.int32, (n, n), 0)
    c = lax.broadcasted_iota(jnp.int32, (n, n), 1)
    causal = (r >= c) if d == 0 else (r <= c)
    acs = _chunk_cumsum(dtc * a_row, causal.astype(BF16))
    last = n - 1 if d == 0 else 0
    to_end = jnp.exp(acs[last:last + 1, :] - acs)
    spread = _select_cols(jnp.concatenate([dtc, jnp.exp(acs), to_end], axis=0), ehead_ref[d])
    dt_x, ea_x, te_x = spread[0:n], spread[n:2 * n], spread[2 * n:]
    xdt = act_ref[rows, 0:D_INNER] * dt_x
    xw = (xdt * te_x).astype(BF16)
    xdt = xdt.astype(BF16)
    gw = D_INNER // D_GROUPS
    heads_per_group = D_HEADS // D_GROUPS
    if want_y:
        acs_t = acs.T
        acs2 = jnp.concatenate(_split2(acs), axis=1)
        lower = lax.broadcasted_iota(jnp.int32, (n, LANE), 1) < D_HEAD_DIM
    y_blocks, carried = [], []
    for g in range(D_GROUPS):
        b0 = D_INNER + g * D_STATE
        c0 = D_INNER + D_GROUPS * D_STATE + g * D_STATE
        bg = act_ref[rows, b0:b0 + D_STATE].astype(BF16)
        st = st_ref[g]
        if want_y:
            cg = act_ref[rows, c0:c0 + D_STATE].astype(BF16)
            gram = _dot_nt(cg, bg)
            carried.append(_dot(cg, st.astype(BF16)))
            for pr in range(heads_per_group // 2):
                pair = g * (heads_per_group // 2) + pr
                pcol = d * (D_HEADS // 2) + pair
                rep = _dot(acs2, erep_ref[pcol])
                weights = []
                for e in range(2):
                    col = 2 * pcol + e
                    decay = jnp.where(causal, jnp.exp(rep[:, e * LANE:(e + 1) * LANE] - acs_t[col:col + 1, :]), 0.0)
                    weights.append((gram * decay).astype(BF16))
                yy = _dot(jnp.concatenate(weights, axis=0), xdt[:, pair * LANE:(pair + 1) * LANE])
                y_blocks.append(jnp.where(lower, yy[0:n], yy[n:]))
        upd = lax.dot_general(bg, xw[:, g * gw:(g + 1) * gw], (((0,), (0,)), ((), ())),
                              preferred_element_type=F32)
        st_ref[g] = ea_x[last:last + 1, g * gw:(g + 1) * gw] * st + upd
    if want_y:
        y = jnp.concatenate(y_blocks, axis=-1) + ea_x * jnp.concatenate(carried, axis=-1)
        y_ref[rows, :] = y_ref[rows, :] + y


def _ssd_body(need_ctx, *refs):
    if need_ctx:
        (xc_ref, dtc_ref, zc_ref, xx_ref, dtx_ref, zx_ref, cw_ref, cb_ref, alog_ref, dskip_ref, mn_ref,
         ehead_ref, erep_ref, oc_ref, ox_ref, actc_ref, actx_ref, yc_ref, yx_ref, st_ref) = refs
    else:
        (xc_ref, dtc_ref, xx_ref, dtx_ref, zx_ref, cw_ref, cb_ref, alog_ref, dskip_ref, mn_ref,
         ehead_ref, erep_ref, ox_ref, actc_ref, actx_ref, yx_ref, st_ref) = refs
        zc_ref = oc_ref = yc_ref = None
    for src, act in ((xc_ref, actc_ref), (xx_ref, actx_ref)):
        for blk in range(ODD_XBC // LANE):
            sl = slice(blk * LANE, (blk + 1) * LANE)
            v = _short_conv(src[0, :, sl].astype(F32), cw_ref.at[:, sl], cb_ref.at[:, sl])
            act[:, sl] = _silu(v)
    a_row = -jnp.exp(alog_ref[...])
    nc_c = actc_ref.shape[0] // D_CHUNK
    nc_x = actx_ref.shape[0] // D_CHUNK
    st_ref[...] = jnp.zeros_like(st_ref)
    yx_ref[...] = jnp.zeros_like(yx_ref)
    if need_ctx:
        yc_ref[...] = jnp.zeros_like(yc_ref)

    def ctx_step(k, carry):
        for d, kk in ((0, k), (1, nc_c - 1 - k)):
            _ssd_chunk(actc_ref, dtc_ref, yc_ref, st_ref.at[d], ehead_ref, erep_ref,
                       pl.multiple_of(kk * D_CHUNK, D_CHUNK), d, a_row, need_ctx)
        return carry

    def lat_step(k, carry):
        for d, kk in ((0, k), (1, nc_x - 1 - k)):
            _ssd_chunk(actx_ref, dtx_ref, yx_ref, st_ref.at[d], ehead_ref, erep_ref,
                       pl.multiple_of(kk * D_CHUNK, D_CHUNK), d, a_row, True)
        return carry

    lax.fori_loop(0, nc_c, ctx_step, 0)
    lax.fori_loop(0, nc_x, lat_step, 0)

    def finish(act, y_ref, z_ref, o_ref):
        def step(k, carry):
            rows = pl.ds(pl.multiple_of(k * D_CHUNK, D_CHUNK), D_CHUNK)
            y = y_ref[rows, :] + act[rows, 0:D_INNER] * dskip_ref[...]
            yz = y * _silu(z_ref[0, rows, :].astype(F32))
            o_ref[0, rows, :] = (_rms_rows(yz) * mn_ref[...]).astype(BF16)
            return carry
        lax.fori_loop(0, act.shape[0] // D_CHUNK, step, 0)

    finish(actx_ref, yx_ref, zx_ref, ox_ref)
    if need_ctx:
        finish(actc_ref, yc_ref, zc_ref, oc_ref)


def _ssd_selectors():
    lane = jnp.arange(LANE)[:, None]
    head = jnp.arange(D_INNER)[None, :] // D_HEAD_DIM
    ehead = jnp.stack([lane == d * D_HEADS + head for d in range(2)])
    pair_col = 2 * jnp.arange(ODD_DT // 2)[:, None, None] + jnp.arange(2 * LANE)[None, None, :] // LANE
    erep = lane[None] == pair_col
    twice = lambda sel: jnp.concatenate([sel, sel], axis=1).astype(BF16)
    return twice(ehead), twice(erep)


def _ssd(ctx_in, lat_in, conv_w, conv_b, alog, dskip, mnorm, need_ctx):
    xc, dtc, zc = ctx_in
    xx, dtx, zx = lat_in
    bsz, n, _ = xx.shape
    nc = xc.shape[1]
    blk = lambda a: pl.BlockSpec((1,) + a.shape[1:], lambda b: (b, 0, 0))
    seqs = [xc, dtc] + ([zc] if need_ctx else []) + [xx, dtx, zx]
    consts = [conv_w, conv_b, alog, dskip, mnorm, *_ssd_selectors()]
    out_x = jax.ShapeDtypeStruct((bsz, n, D_INNER), BF16)
    out_c = jax.ShapeDtypeStruct((bsz, nc, D_INNER), BF16)
    out_shape = [out_c, out_x] if need_ctx else [out_x]
    scratch = [pltpu.VMEM((nc, ODD_XBC), F32), pltpu.VMEM((n, ODD_XBC), F32)]
    scratch += ([pltpu.VMEM((nc, D_INNER), F32)] if need_ctx else []) + [pltpu.VMEM((n, D_INNER), F32)]
    scratch += [pltpu.VMEM((2, D_GROUPS, D_STATE, D_INNER // D_GROUPS), F32)]
    outs = pl.pallas_call(
        functools.partial(_ssd_body, need_ctx),
        grid=(bsz,),
        in_specs=[blk(a) for a in seqs] + [_const_spec(c.shape) for c in consts],
        out_specs=[blk(o) for o in out_shape],
        out_shape=out_shape,
        scratch_shapes=scratch,
        compiler_params=_params("parallel"),
        name="ssd",
    )(*seqs, *consts)
    return (outs[0], outs[1]) if need_ctx else (None, outs[0])


def _axial_rope(n_tokens, rot_dim):
    rows = n_tokens // GRID_W
    r_idx, c_idx = jnp.meshgrid(jnp.arange(rows), jnp.arange(GRID_W), indexing="ij")
    quarter = rot_dim // 4
    inv_freq = ROPE_THETA ** (-jnp.arange(quarter, dtype=F32) / quarter)
    ang = jnp.concatenate([r_idx.reshape(-1, 1).astype(F32) * inv_freq,
                           c_idx.reshape(-1, 1).astype(F32) * inv_freq], axis=-1)
    return jnp.cos(ang), jnp.sin(ang)


def _rope_tables(n_tokens, rotate):
    ones, zeros = jnp.ones((n_tokens, 64), F32), jnp.zeros((n_tokens, 64), F32)
    if rotate:
        ca, sa = _axial_rope(n_tokens, A_HEAD_DIM)
        cb, sb = _axial_rope(n_tokens, B_ROPE_DIM)
    else:
        ca, sa = jnp.ones((n_tokens, 32), F32), jnp.zeros((n_tokens, 32), F32)
        cb, sb = jnp.ones((n_tokens, 16), F32), jnp.zeros((n_tokens, 16), F32)
    z32 = zeros[:, :32]
    return (jnp.tile(jnp.concatenate([ca, ca], -1), (1, 2)),
            jnp.tile(jnp.concatenate([-sa, sa], -1), (1, 2)),
            jnp.concatenate([ones, cb, cb, z32], -1),
            jnp.concatenate([zeros, -sb, sb, z32], -1))


def _even_weights(w_in, a_qn, a_kn, b_qn, b_kvn, b_wq, b_wkv):
    d = w_in.shape[0]
    ak, av, bkva, bkr, aq, bqa = jnp.split(w_in, [128, 256, 384, 416, 928], axis=1)
    z64, z32 = jnp.zeros((d, 64), F32), jnp.zeros((d, 32), F32)
    half = B_ROPE_DIM // 2
    kr = jnp.concatenate([z64, bkr, z32], 1)
    krs = jnp.concatenate([z64, bkr[:, half:], bkr[:, :half], z32], 1)
    av0 = jnp.concatenate([av[:, :A_HEAD_DIM], z64], 1)
    av1 = jnp.concatenate([av[:, A_HEAD_DIM:], z64], 1)
    win = jnp.concatenate([aq, bqa, ak, av0, av1, bkva, kr, krs], 1).astype(BF16)
    wq = b_wq.reshape(B_Q_RANK, B_HEADS, B_NOPE_DIM + B_ROPE_DIM)
    nope, r1, r2 = wq[..., :B_NOPE_DIM], wq[..., B_NOPE_DIM:B_NOPE_DIM + half], wq[..., B_NOPE_DIM + half:]
    zq64, zq32 = jnp.zeros_like(nope), jnp.zeros((B_Q_RANK, B_HEADS, 32), F32)
    wq1 = jnp.concatenate([nope, r1, r2, zq32], -1).reshape(B_Q_RANK, B_HEADS * B_PAD_DIM)
    wq2 = jnp.concatenate([zq64, r2, r1, zq32], -1).reshape(B_Q_RANK, B_HEADS * B_PAD_DIM)
    wkv = b_wkv.reshape(B_KV_RANK, B_HEADS, B_NOPE_DIM + B_V_DIM)
    wk = jnp.concatenate([wkv[..., :B_NOPE_DIM], jnp.zeros((B_KV_RANK, B_HEADS, 64), F32)], -1)
    wk = wk.reshape(B_KV_RANK, B_HEADS * B_PAD_DIM)
    wv = jnp.concatenate([wkv[..., B_NOPE_DIM:], jnp.zeros((B_KV_RANK, B_HEADS, 64), F32)], -1)
    wv = wv.reshape(B_KV_RANK, B_HEADS * LANE)
    blockdiag = lambda n: (jnp.arange(n)[:, None] // A_HEAD_DIM == jnp.arange(n)[None, :] // A_HEAD_DIM)
    return dict(
        win=win,
        wq=jnp.concatenate([wq1, wq2], 1).astype(BF16),
        wkv=jnp.concatenate([wk, wv], 1).astype(BF16),
        gqa=jnp.tile(a_qn, A_HEADS).reshape(1, -1), gka=jnp.tile(a_kn, A_KV_HEADS).reshape(1, -1),
        gqb=b_qn.reshape(1, -1), gkvb=b_kvn.reshape(1, -1),
        bdq=(blockdiag(A_HEADS * A_HEAD_DIM) / A_HEAD_DIM).astype(BF16),
        bdk=(blockdiag(A_KV_HEADS * A_HEAD_DIM) / A_HEAD_DIM).astype(BF16))


def kernel(x, c, ctx, c_ctx, ada_w, ada_b, norm_g, ff_w1, ff_w2, at_w_in, at_w_out, ga_q_norm, ga_k_norm,
           ml_q_norm, ml_kv_norm, ml_w_q_up, ml_w_kv_up, lc_w_in, lc_w_out, hy_short_w, hy_short_b, hy_w1,
           hy_b1, hy_w2, hy_b2, hy_w3, hy_freq, hy_skip, mb_conv_w, mb_conv_b, mb_a_log, mb_dt_bias, mb_d_skip,
           mb_norm):
    bsz, n_lat, d = x.shape
    n_ctx = ctx.shape[1]
    depth = ada_w.shape[0]
    rows = -(-(bsz + 1) // 8) * 8
    cond = jnp.concatenate([c, c_ctx[None], jnp.zeros((rows - bsz - 1, d), F32)], axis=0)
    mods = _ada_table(cond, ada_w, ada_b)
    rope_x = _rope_tables(n_lat, True)
    rope_c = _rope_tables(n_ctx, False)
    tm = 512
    for i in range(depth):
        need_ctx = i < depth - 1
        j = i // 2
        g = norm_g[i]
        mod_x = mods[i, :bsz].reshape(bsz, 6, d)
        mod_c = mods[i, bsz:bsz + 1].reshape(1, 6, d)
        if i % 2 == 0:
            ew = _even_weights(at_w_in[j], ga_q_norm[j], ga_k_norm[j], ml_q_norm[j], ml_kv_norm[j],
                               ml_w_q_up[j], ml_w_kv_up[j])
            px = _even_prep(x, mod_x, g, ew, rope_x, tm)
            pc = _even_prep(ctx, mod_c, g, ew, rope_c, tm)
            o1x, o2x = _attention(px[0:4:3], [px[1:3] + px[4:6], pc[1:3] + pc[4:6]], 256)
            if need_ctx:
                o1c, o2c = _attention(pc[0:4:3], [pc[1:3] + pc[4:6]], 256)
            w_out = at_w_out[j].astype(BF16)
        else:
            pad = jnp.zeros((d, LANE - ODD_DT), F32)
            win = jnp.concatenate([lc_w_in[j], pad], axis=1).astype(BF16)
            dtb = jnp.pad(mb_dt_bias[j].reshape(-1), (0, LANE - ODD_DT)).reshape(1, LANE)
            hy_x, z_x, xbc_x, dt_x = _odd_prep(x, mod_x, g, win, dtb, tm)
            hy_c, z_c, xbc_c, dt_c = _odd_prep(ctx, mod_c, g, win, dtb, tm)
            filt = (hy_w1[j], hy_b1[j], hy_w2[j], hy_b2[j], hy_w3[j], hy_freq[j])
            sw, sb = hy_short_w[j], hy_short_b[j].reshape(1, -1)
            o1x = _hyena(hy_x, sw, sb, hy_skip[j], filt)
            alog = jnp.pad(mb_a_log[j].reshape(-1), (0, LANE - ODD_DT)).reshape(1, LANE)
            dskip = jnp.repeat(mb_d_skip[j], D_HEAD_DIM).reshape(1, D_INNER)
            o2c, o2x = _ssd((xbc_c, dt_c, z_c), (xbc_x, dt_x, z_x), mb_conv_w[j], mb_conv_b[j].reshape(1, -1),
                            alog, dskip, mb_norm[j].reshape(1, -1), need_ctx)
            if need_ctx:
                o1c = _hyena(hy_c, sw, sb, hy_skip[j], filt)
            w_out = lc_w_out[j].astype(BF16)
        w1, w2 = ff_w1[i].astype(BF16), ff_w2[i].astype(BF16)
        x = _post(x, o1x, o2x, mod_x, g, w_out[:MIX_HALF], w_out[MIX_HALF:], w1, w2, tm)
        if need_ctx:
            ctx = _post(ctx, o1c, o2c, mod_c, g, w_out[:MIX_HALF], w_out[MIX_HALF:], w1, w2, tm)
    return x
```

```python
import functools
import math

import jax
import jax.numpy as jnp
from jax import lax
from jax.experimental import pallas as pl
from jax.experimental.pallas import tpu as pltpu

F32 = jnp.float32
BF16 = jnp.bfloat16

D_MODEL = 1024
GRID_W = 64
EPS = 1e-6
ROPE_THETA = 10000.0
MIX_HALF = D_MODEL // 2
A_HEAD_DIM = 64
A_HEADS = MIX_HALF // A_HEAD_DIM
A_KV_HEADS = A_HEADS // 4
A_GROUP = A_HEADS // A_KV_HEADS
B_NOPE_DIM = 64
B_ROPE_DIM = 32
B_V_DIM = 64
B_HEADS = MIX_HALF // B_V_DIM
B_Q_RANK = D_MODEL // 4
B_KV_RANK = D_MODEL // 8
B_PAD_DIM = 128
C_WIDTH = MIX_HALF
C_ORDER = 2
C_POS_EMB = 33
C_FILTER_HIDDEN = 64
C_DECAY_TARGET = 1e-2
C_DECAY_FRAC_SHORT = 0.3
C_DECAY_FRAC_LONG = 1.5
D_INNER = MIX_HALF
D_HEAD_DIM = 64
D_HEADS = D_INNER // D_HEAD_DIM
D_GROUPS = 2
D_STATE = 128
D_CHUNK = 128
FF_HIDDEN = 4 * D_MODEL
FF_CHUNK = 1024
A_SCALE = A_HEAD_DIM ** -0.5
B_SCALE = (B_NOPE_DIM + B_ROPE_DIM) ** -0.5
LOG2E = math.log2(math.e)
V_DIM = 64
ODD_HY = (C_ORDER + 1) * C_WIDTH
ODD_Z = D_INNER
ODD_XBC = D_INNER + 2 * D_GROUPS * D_STATE
ODD_DT = 2 * D_HEADS
LANE = 128
VMEM_LIMIT = 56 * 2 ** 20


def _params(*sem):
    return pltpu.CompilerParams(dimension_semantics=sem, vmem_limit_bytes=VMEM_LIMIT)


def _const_spec(shape):
    nd = len(shape)
    return pl.BlockSpec(shape, lambda *_: (0,) * nd, pipeline_mode=pl.Buffered(1))


def _dot(a, b):
    return jnp.dot(a, b, preferred_element_type=F32)


def _dot_nt(a, b):
    return lax.dot_general(a, b, (((1,), (1,)), ((), ())), preferred_element_type=F32)


def _dot_f32(a, b):
    return jnp.dot(a, b, preferred_element_type=F32, precision=lax.Precision.HIGHEST)


def _rms_rows(x):
    return x * lax.rsqrt(jnp.mean(x * x, axis=-1, keepdims=True) + EPS)


def _silu(x):
    return x * jax.nn.sigmoid(x)


def _swap_halves(x, half):
    n = x.shape[-1]
    lane = lax.broadcasted_iota(jnp.int32, x.shape, 1)
    r1 = pltpu.roll(x, half, 1)
    r2 = pltpu.roll(x, n - half, 1)
    i1 = pltpu.roll(lane, half, 1)
    return jnp.where(i1 == (lane ^ half), r1, r2)


def _row_neighbours(u):
    n = u.shape[0]
    row = lax.broadcasted_iota(jnp.int32, u.shape, 0)
    prev = jnp.where(row == 0, 0.0, pltpu.roll(u, 1, 0))
    nxt = jnp.where(row == n - 1, 0.0, pltpu.roll(u, n - 1, 0))
    return prev, nxt


def _short_conv(u, w_ref, b_ref):
    prev, nxt = _row_neighbours(u)
    return w_ref[0:1, :] * prev + w_ref[1:2, :] * u + w_ref[2:3, :] * nxt + b_ref[...]


def _ada_body(cond_ref, w_ref, b_ref, o_ref):
    h = _silu(cond_ref[...]).astype(BF16)
    o_ref[0] = _dot(h, w_ref[0].astype(BF16)) + b_ref[0]


def _ada_table(cond, ada_w, ada_b):
    depth, d, n = ada_w.shape
    r = cond.shape[0]
    tn = 1536
    return pl.pallas_call(
        _ada_body,
        grid=(depth, n // tn),
        in_specs=[pl.BlockSpec((r, d), lambda l, j: (0, 0)),
                  pl.BlockSpec((1, d, tn), lambda l, j: (l, 0, j)),
                  pl.BlockSpec((1, 1, tn), lambda l, j: (l, 0, j))],
        out_specs=pl.BlockSpec((1, r, tn), lambda l, j: (l, 0, j)),
        out_shape=jax.ShapeDtypeStruct((depth, r, n), F32),
        compiler_params=_params("arbitrary", "arbitrary"),
        name="ada_table",
    )(cond, ada_w, ada_b.reshape(depth, 1, n))


def _mod_index(per_batch):
    return (lambda b, i: (b, 0, 0)) if per_batch else (lambda b, i: (0, 0, 0))


def _norm_mod(x, g_row, mod, shift_row, scale_row):
    h = _rms_rows(x) * g_row
    return h * (1.0 + mod[scale_row:scale_row + 1]) + mod[shift_row:shift_row + 1]


def _even_prep_body(s_ref, mod_ref, g_ref, win_ref, wq_ref, wkv_ref, gqa_ref, gka_ref, gqb_ref, gkvb_ref,
                    bdq_ref, bdk_ref, ca_ref, sa_ref, cb_ref, sb_ref,
                    qa_ref, ka_ref, va_ref, qb_ref, kb_ref, vb_ref):
    h = _norm_mod(s_ref[0], g_ref[0:1], mod_ref[0], 0, 1).astype(BF16)
    p = _dot(h, win_ref[...])
    aq, bqa = p[:, 0:512], p[:, 512:768]
    ak, av, bkva = p[:, 768:896], p[:, 896:1152], p[:, 1152:1280]
    kr, krs = p[:, 1280:1408], p[:, 1408:1536]
    ca, sa, cb, sb = ca_ref[...], sa_ref[...], cb_ref[...], sb_ref[...]
    upper = lax.broadcasted_iota(jnp.int32, (h.shape[0], LANE), 1) >= V_DIM

    aqn = aq * lax.rsqrt(_dot((aq * aq).astype(BF16), bdq_ref[...]) + EPS) * gqa_ref[...]
    aqs = _swap_halves(aqn, A_HEAD_DIM // 2)
    for blk in range(4):
        sl = slice(blk * LANE, (blk + 1) * LANE)
        q2 = ((aqn[:, sl] * ca + aqs[:, sl] * sa) * (A_SCALE * LOG2E)).astype(BF16)
        qa_ref[0, 2 * blk] = q2[:, :A_HEAD_DIM]
        qa_ref[0, 2 * blk + 1] = q2[:, A_HEAD_DIM:]
    akn = ak * lax.rsqrt(_dot((ak * ak).astype(BF16), bdk_ref[...]) + EPS) * gka_ref[...]
    akr = (akn * ca + _swap_halves(akn, A_HEAD_DIM // 2) * sa).astype(BF16)
    for j in range(A_KV_HEADS):
        ka_ref[0, j] = akr[:, j * A_HEAD_DIM:(j + 1) * A_HEAD_DIM]
        va_ref[0, j] = jnp.where(upper, 1.0, av[:, j * LANE:(j + 1) * LANE]).astype(BF16)

    bqn = (_rms_rows(bqa) * gqb_ref[...]).astype(BF16)
    nq = B_HEADS * B_PAD_DIM
    u = _dot(bqn, wq_ref[:, :nq])
    us = _dot(bqn, wq_ref[:, nq:])
    bkvn = (_rms_rows(bkva) * gkvb_ref[...]).astype(BF16)
    uk = _dot(bkvn, wkv_ref[:, :nq])
    uv = _dot(bkvn, wkv_ref[:, nq:])
    krr = kr * cb + krs * sb
    for hd in range(B_HEADS):
        sl = slice(hd * B_PAD_DIM, (hd + 1) * B_PAD_DIM)
        qb_ref[0, hd] = ((u[:, sl] * cb + us[:, sl] * sb) * (B_SCALE * LOG2E)).astype(BF16)
        kb_ref[0, hd] = (uk[:, sl] + krr).astype(BF16)
        vb_ref[0, hd] = jnp.where(upper, 1.0, uv[:, sl]).astype(BF16)


def _even_prep(s, mod, g, ew, rope, tm):
    bsz, t, d = s.shape
    tm = min(tm, t)
    per_batch = mod.shape[0] != 1
    row = lambda b, i: (b, i, 0)
    hrow = lambda b, i: (b, 0, i, 0)
    tab = lambda b, i: (i, 0)
    consts = [ew["win"], ew["wq"], ew["wkv"], ew["gqa"], ew["gka"], ew["gqb"], ew["gkvb"], ew["bdq"], ew["bdk"]]
    outs = [(A_HEADS, A_HEAD_DIM), (A_KV_HEADS, A_HEAD_DIM), (A_KV_HEADS, LANE),
            (B_HEADS, B_PAD_DIM), (B_HEADS, B_PAD_DIM), (B_HEADS, LANE)]
    return pl.pallas_call(
        _even_prep_body,
        grid=(bsz, t // tm),
        in_specs=[pl.BlockSpec((1, tm, d), row),
                  pl.BlockSpec((1, 6, d), _mod_index(per_batch)),
                  _const_spec(g.shape)]
        + [_const_spec(c.shape) for c in consts]
        + [pl.BlockSpec((tm, LANE), tab)] * 4,
        out_specs=[pl.BlockSpec((1, nh, tm, hd), hrow) for nh, hd in outs],
        out_shape=[jax.ShapeDtypeStruct((bsz, nh, t, hd), BF16) for nh, hd in outs],
        compiler_params=_params("parallel", "arbitrary"),
        name="even_prep",
    )(s, mod, g, *consts, *rope)


def _softmax_attend(q, kvs):
    ss = [_dot_nt(q, k) for k, _ in kvs]
    m = ss[0].max(axis=-1, keepdims=True)
    for s in ss[1:]:
        m = jnp.maximum(m, s.max(axis=-1, keepdims=True))
    acc = None
    for s, (_, v) in zip(ss, kvs):
        o = _dot(jnp.exp2(s - m).astype(BF16), v)
        acc = o if acc is None else acc + o
    return acc / pltpu.roll(acc, V_DIM, 1)


def _attn_body(n_src, qa_ref, qb_ref, *refs):
    kv_refs = refs[:4 * n_src]
    oa_ref, ob_ref = refs[4 * n_src:]
    srcs = [kv_refs[4 * i:4 * i + 4] for i in range(n_src)]
    lower = lax.broadcasted_iota(jnp.int32, (qa_ref.shape[2], LANE), 1) < V_DIM

    def pair_out(first, second):
        return jnp.where(lower, first, pltpu.roll(second, V_DIM, 1)).astype(BF16)

    for pair in range(A_HEADS // 2):
        outs = []
        for hd in (2 * pair, 2 * pair + 1):
            j = hd // A_GROUP
            outs.append(_softmax_attend(qa_ref[0, hd], [(ka[0, j], va[0, j]) for ka, va, _, _ in srcs]))
        oa_ref[0, :, pair * LANE:(pair + 1) * LANE] = pair_out(*outs)
    for pair in range(B_HEADS // 2):
        outs = []
        for hd in (2 * pair, 2 * pair + 1):
            outs.append(_softmax_attend(qb_ref[0, hd], [(kb[0, hd], vb[0, hd]) for _, _, kb, vb in srcs]))
        ob_ref[0, :, pair * LANE:(pair + 1) * LANE] = pair_out(*outs)


def _attention(q, kv_srcs, tq):
    qa, qb = q
    bsz, _, t, _ = qa.shape
    tq = min(tq, t)
    qrow = lambda b, i: (b, 0, i, 0)
    whole = lambda b, i: (b, 0, 0, 0)
    in_specs = [pl.BlockSpec((1, A_HEADS, tq, A_HEAD_DIM), qrow), pl.BlockSpec((1, B_HEADS, tq, B_PAD_DIM), qrow)]
    args = [qa, qb]
    for src in kv_srcs:
        for a in src:
            in_specs.append(pl.BlockSpec((1,) + a.shape[1:], whole))
            args.append(a)
    out = jax.ShapeDtypeStruct((bsz, t, MIX_HALF), BF16)
    return pl.pallas_call(
        functools.partial(_attn_body, len(kv_srcs)),
        grid=(bsz, t // tq),
        in_specs=in_specs,
        out_specs=[pl.BlockSpec((1, tq, MIX_HALF), lambda b, i: (b, i, 0))] * 2,
        out_shape=[out, out],
        compiler_params=_params("parallel", "arbitrary"),
        name="attention",
    )(*args)


def _post_body(s_ref, o1_ref, o2_ref, mod_ref, g_ref, wo1_ref, wo2_ref, w1_ref, w2_ref, out_ref):
    mod = mod_ref[0]
    mo = _dot(o1_ref[0], wo1_ref[...]) + _dot(o2_ref[0], wo2_ref[...])
    x1 = s_ref[0] + mod[2:3] * (_rms_rows(mo) * g_ref[1:2])
    h = _norm_mod(x1, g_ref[2:3], mod, 3, 4).astype(BF16)
    acc = None
    for c in range(FF_HIDDEN // FF_CHUNK):
        sl = slice(c * FF_CHUNK, (c + 1) * FF_CHUNK)
        u = jnp.square(jnp.maximum(_dot(h, w1_ref[:, sl]), 0.0)).astype(BF16)
        part = _dot(u, w2_ref[sl, :])
        acc = part if acc is None else acc + part
    out_ref[0] = x1 + mod[5:6] * (_rms_rows(acc) * g_ref[3:4])


def _post(s, o1, o2, mod, g, wo1, wo2, w1, w2, tm, in_place):
    bsz, t, d = s.shape
    tm = min(tm, t)
    per_batch = mod.shape[0] != 1
    row = lambda b, i: (b, i, 0)
    return pl.pallas_call(
        _post_body,
        grid=(bsz, t // tm),
        in_specs=[pl.BlockSpec((1, tm, d), row),
                  pl.BlockSpec((1, tm, MIX_HALF), row),
                  pl.BlockSpec((1, tm, MIX_HALF), row),
                  pl.BlockSpec((1, 6, d), _mod_index(per_batch)),
                  _const_spec(g.shape), _const_spec(wo1.shape), _const_spec(wo2.shape),
                  _const_spec(w1.shape), _const_spec(w2.shape)],
        out_specs=pl.BlockSpec((1, tm, d), row),
        out_shape=jax.ShapeDtypeStruct(s.shape, F32),
        input_output_aliases={0: 0} if in_place else {},
        compiler_params=_params("parallel", "arbitrary"),
        name="post",
    )(s, o1, o2, mod, g, wo1, wo2, w1, w2)


ODD_NIN = ODD_HY + ODD_Z + ODD_XBC + LANE


def _softplus(x):
    return jnp.maximum(x, 0.0) + jnp.log1p(jnp.exp(-jnp.abs(x)))


def _odd_prep_body(s_ref, mod_ref, g_ref, win_ref, dtb_ref, hy_ref, z_ref, xbc_ref, dt_ref):
    h = _norm_mod(s_ref[0], g_ref[0:1], mod_ref[0], 0, 1).astype(BF16)
    p = _dot(h, win_ref[...])
    hy_ref[0] = p[:, :ODD_HY].astype(BF16)
    z_ref[0] = p[:, ODD_HY:ODD_HY + ODD_Z].astype(BF16)
    xbc_ref[0] = p[:, ODD_HY + ODD_Z:ODD_HY + ODD_Z + ODD_XBC].astype(BF16)
    dt_ref[0] = _softplus(p[:, ODD_HY + ODD_Z + ODD_XBC:] + dtb_ref[...])


def _odd_prep(s, mod, g, win, dtb, tm):
    bsz, t, d = s.shape
    tm = min(tm, t)
    per_batch = mod.shape[0] != 1
    row = lambda b, i: (b, i, 0)
    widths = [(ODD_HY, BF16), (ODD_Z, BF16), (ODD_XBC, BF16), (LANE, F32)]
    return pl.pallas_call(
        _odd_prep_body,
        grid=(bsz, t // tm),
        in_specs=[pl.BlockSpec((1, tm, d), row),
                  pl.BlockSpec((1, 6, d), _mod_index(per_batch)),
                  _const_spec(g.shape), _const_spec(win.shape), _const_spec(dtb.shape)],
        out_specs=[pl.BlockSpec((1, tm, w), row) for w, _ in widths],
        out_shape=[jax.ShapeDtypeStruct((bsz, t, w), dt) for w, dt in widths],
        compiler_params=_params("parallel", "arbitrary"),
        name="odd_prep",
    )(s, mod, g, win, dtb)


def _filter_body(z_ref, w1_ref, b1_ref, w2_ref, b2_ref, w3_ref, fr_ref, dec_ref, o_ref):
    fr = fr_ref[...]
    h = jnp.sin(fr * (_dot_f32(z_ref[...], w1_ref[...]) + b1_ref[...]))
    h = jnp.sin(fr * (_dot_f32(h, w2_ref[...]) + b2_ref[...]))
    o_ref[...] = _dot_f32(h, w3_ref[...]) * dec_ref[...]


def _hyena_taps(n, w1, b1, w2, b2, w3, freq):
    t = jnp.linspace(0.0, 1.0, n, dtype=F32)
    bands = (C_POS_EMB - 1) // 2
    w = 2.0 * math.pi * jnp.arange(n, dtype=F32) / n
    fb = jnp.linspace(1e-4, bands - 1, bands, dtype=F32)
    ph = w[:, None] * fb[None, :]
    z = jnp.concatenate([t[:, None], jnp.cos(ph), -jnp.sin(ph)], axis=-1)
    decay_max = math.log(C_DECAY_TARGET) / C_DECAY_FRAC_SHORT
    decay_min = math.log(C_DECAY_TARGET) / C_DECAY_FRAC_LONG
    deltas = jnp.abs(jnp.linspace(decay_min, decay_max, C_WIDTH, dtype=F32))
    dec = jnp.tile(jnp.exp(-t[:, None] * deltas[None, :]), (1, 2 * C_ORDER))
    pad_h = LANE - C_FILTER_HIDDEN
    zp = jnp.pad(z, ((0, 0), (0, LANE - C_POS_EMB)))
    w1p = jnp.pad(w1, ((0, LANE - C_POS_EMB), (0, pad_h)))
    w2p = jnp.pad(w2, ((0, pad_h), (0, pad_h)))
    w3p = jnp.pad(w3, ((0, pad_h), (0, 0)))
    row = lambda v: jnp.pad(v, (0, pad_h)).reshape(1, LANE)
    nout = 2 * C_ORDER * C_WIDTH
    tn = 512
    tr = min(n, 512)
    return pl.pallas_call(
        _filter_body,
        grid=(n // tr, nout // tn),
        in_specs=[pl.BlockSpec((tr, LANE), lambda i, j: (i, 0)),
                  pl.BlockSpec((LANE, LANE), lambda i, j: (0, 0)),
                  pl.BlockSpec((1, LANE), lambda i, j: (0, 0)),
                  pl.BlockSpec((LANE, LANE), lambda i, j: (0, 0)),
                  pl.BlockSpec((1, LANE), lambda i, j: (0, 0)),
                  pl.BlockSpec((LANE, tn), lambda i, j: (0, j)),
                  pl.BlockSpec((1, LANE), lambda i, j: (0, 0)),
                  pl.BlockSpec((tr, tn), lambda i, j: (i, j))],
        out_specs=pl.BlockSpec((tr, tn), lambda i, j: (i, j)),
        out_shape=jax.ShapeDtypeStruct((n, nout), F32),
        compiler_params=_params("arbitrary", "arbitrary"),
        name="hyena_taps",
    )(zp, w1p, row(b1), w2p, row(b2), w3p, row(freq), dec)


def _dft_matrices(n):
    m = 2 * n
    k = jnp.arange(n, dtype=jnp.int32)[:, None]
    s = jnp.arange(n, dtype=jnp.int32)[None, :]
    ang = ((k * s) % m).astype(F32) * (2.0 * math.pi / m)
    nyq = jnp.where(s % 2 == 0, 1.0, -1.0).astype(F32)
    cosm = jnp.cos(ang)
    sinm = jnp.where(k == 0, nyq, -jnp.sin(ang))
    fwd = jnp.concatenate([cosm, sinm], axis=0)
    wgt = jnp.where(k == 0, 1.0 / m, 2.0 / m)
    inv = jnp.concatenate([(cosm * wgt).T, (sinm * wgt).T], axis=1)
    return fwd.astype(BF16), inv.astype(BF16)


DFT_ROWS = 256


def _first_row(shape, offset=0):
    return lax.broadcasted_iota(jnp.int32, shape, 0) + offset == 0


def _dft_row_tiles(n, step):
    tr = min(DFT_ROWS, n)

    def body(r, carry):
        r0 = pl.multiple_of(r * tr, tr)
        step(r0, pl.ds(r0, tr), pl.ds(pl.multiple_of(n + r0, tr), tr))
        return carry
    lax.fori_loop(0, n // tr, body, 0)


def _spectrum_body(n, f_ref, fw_ref, bw_ref, o_ref, fs_ref, bs_ref):
    fw = fw_ref[0]
    bw = jnp.where(_first_row(fw.shape), 0.0, bw_ref[0])
    for u, s_ref in ((fw, fs_ref), (bw, bs_ref)):
        hi = u.astype(BF16)
        s_ref[0] = hi
        s_ref[1] = (u - hi.astype(F32)).astype(BF16)

    def step(r0, rows, rows_im):
        def packed(s_ref, rr):
            return _dot(f_ref[rr, :], s_ref[0]) + _dot(f_ref[rr, :], s_ref[1])
        o_ref[0, 0, rows, :] = packed(fs_ref, rows) + packed(bs_ref, rows)
        im1, im2 = packed(fs_ref, rows_im), packed(bs_ref, rows_im)
        o_ref[0, 0, rows_im, :] = jnp.where(_first_row(im1.shape, r0), im1 + im2, im1 - im2)
    _dft_row_tiles(n, step)


def _two_sided_taps(taps):
    n = taps.shape[0]
    t4 = taps.reshape(n, C_ORDER, 2, C_WIDTH)
    fwd = jnp.moveaxis(t4[:, :, 0], 1, 0)
    bwd = jnp.moveaxis(t4[:, :, 1], 1, 0)
    zero = jnp.zeros_like(fwd[:, :1])
    ts = jnp.concatenate([zero, jnp.flip(bwd[:, 1:], axis=1), fwd], axis=1)
    rs = jnp.concatenate([zero, jnp.flip(fwd[:, 1:], axis=1), fwd[:, :1], bwd[:, 1:]], axis=1)
    return ts, rs


def _hyena_spectrum(ts, rs, fwd, tb, cw):
    n = ts.shape[1] // 2
    nb = n // tb
    nd = 2 * nb - 1
    return pl.pallas_call(
        functools.partial(_spectrum_body, tb),
        grid=(C_ORDER, nd, C_WIDTH // cw),
        in_specs=[_const_spec(fwd.shape),
                  pl.BlockSpec((1, tb, cw), lambda o, di, j: (o, di + 1, j)),
                  pl.BlockSpec((1, tb, cw), lambda o, di, j: (o, 2 * nb - 1 - di, j))],
        out_specs=pl.BlockSpec((1, 1, 2 * tb, cw), lambda o, di, j: (o, di, 0, j)),
        out_shape=jax.ShapeDtypeStruct((C_ORDER, nd, 2 * tb, C_WIDTH), F32),
        scratch_shapes=[pltpu.VMEM((2, tb, cw), BF16), pltpu.VMEM((2, tb, cw), BF16)],
        compiler_params=_params("arbitrary", "arbitrary", "arbitrary"),
        name="hyena_spectrum",
    )(fwd, ts, rs)


HY_BLOCK = 512
HY_TILE = 64


def _hy_conv_body(n, tb, prev_conv, prev_ref, pw_ref, pb_ref, gate_ref, gw_ref, gb_ref, skip_ref, f_ref, g_ref,
                  h_ref, o_ref, zb_ref, u_ref, yb_ref, gate_s, skip_s):
    nb = n // tb
    prev = prev_ref[0].astype(F32)
    if prev_conv:
        prev = _short_conv(prev, pw_ref, pb_ref)
    zb_ref[...] = prev.astype(BF16)
    gate = _short_conv(gate_ref[0].astype(F32), gw_ref, gb_ref)
    gate_s[...] = gate
    skip_s[...] = gate * (prev * skip_ref[0])
    for j in range(nb):
        u_ref[j] = _dot(f_ref[...], zb_ref[j * tb:(j + 1) * tb, :])

    def mac_tile(i, r0):
        rows, rows_im = pl.ds(r0, HY_TILE), pl.ds(tb + r0, HY_TILE)
        re_acc = im_acc = None
        for j in range(nb):
            di = i - j + nb - 1
            ur, ui = u_ref[j, rows, :], u_ref[j, rows_im, :]
            hr, hi = h_ref[0, di, rows, :], h_ref[0, di, rows_im, :]
            imim = ui * hi
            first = _first_row(ur.shape, r0)
            re_t = ur * hr - jnp.where(first, 0.0, imim)
            im_t = jnp.where(first, imim, ur * hi + ui * hr)
            re_acc = re_t if re_acc is None else re_acc + re_t
            im_acc = im_t if im_acc is None else im_acc + im_t
        yb_ref[rows, :] = re_acc.astype(BF16)
        yb_ref[rows_im, :] = im_acc.astype(BF16)

    for i in range(nb):
        def body(r, carry, i=i):
            mac_tile(i, pl.multiple_of(r * HY_TILE, HY_TILE))
            return carry
        lax.fori_loop(0, tb // HY_TILE, body, 0)
        y = _dot(g_ref[...], yb_ref[...])
        sl = slice(i * tb, (i + 1) * tb)
        o_ref[0, sl, :] = (gate_s[sl, :] * y + skip_s[sl, :]).astype(BF16)


def _hy_conv(prev, prev_part, prev_conv, hy, gate_part, conv_w, conv_b, skip, fwd, inv, spec, order, tb, cw):
    bsz, n, _ = hy.shape
    nb = C_WIDTH // cw
    nd = spec.shape[1]
    return pl.pallas_call(
        functools.partial(_hy_conv_body, n, tb, prev_conv),
        grid=(nb, bsz),
        in_specs=[pl.BlockSpec((1, n, cw), lambda j, b: (b, 0, prev_part * nb + j)),
                  pl.BlockSpec((3, cw), lambda j, b: (0, prev_part * nb + j)),
                  pl.BlockSpec((1, cw), lambda j, b: (0, prev_part * nb + j)),
                  pl.BlockSpec((1, n, cw), lambda j, b: (b, 0, gate_part * nb + j)),
                  pl.BlockSpec((3, cw), lambda j, b: (0, gate_part * nb + j)),
                  pl.BlockSpec((1, cw), lambda j, b: (0, gate_part * nb + j)),
                  pl.BlockSpec((1, 1, cw), lambda j, b: (order, 0, j)),
                  _const_spec(fwd.shape), _const_spec(inv.shape),
                  pl.BlockSpec((1, nd, 2 * tb, cw), lambda j, b: (order, 0, 0, j))],
        out_specs=pl.BlockSpec((1, n, cw), lambda j, b: (b, 0, j)),
        out_shape=jax.ShapeDtypeStruct((bsz, n, C_WIDTH), BF16),
        scratch_shapes=[pltpu.VMEM((n, cw), BF16), pltpu.VMEM((n // tb, 2 * tb, cw), F32),
                        pltpu.VMEM((2 * tb, cw), BF16), pltpu.VMEM((n, cw), F32), pltpu.VMEM((n, cw), F32)],
        compiler_params=_params("parallel", "arbitrary"),
        name="hyena_conv",
    )(prev, conv_w, conv_b, hy, conv_w, conv_b, skip, fwd, inv, spec)


def _hyena(hy, conv_w, conv_b, skip, filt):
    n = hy.shape[1]
    tb = min(HY_BLOCK, n)
    cw = 256
    fwd, inv = _dft_matrices(tb)
    spec = _hyena_spectrum(*_two_sided_taps(_hyena_taps(n, *filt)), fwd, tb, cw)
    skip3 = skip.reshape(C_ORDER, 1, C_WIDTH)
    z1 = _hy_conv(hy, 0, True, hy, 1, conv_w, conv_b, skip3, fwd, inv, spec, 0, tb, cw)
    return _hy_conv(z1, 0, False, hy, 2, conv_w, conv_b, skip3, fwd, inv, spec, 1, tb, cw)


def _split2(x):
    hi = x.astype(BF16)
    return hi, (x - hi.astype(F32)).astype(BF16)


def _select_cols(x, sel2):
    return _dot(jnp.concatenate(_split2(x), axis=1), sel2)


def _chunk_cumsum(adt, tri):
    hi = adt.astype(BF16)
    r1 = adt - hi.astype(F32)
    mid = r1.astype(BF16)
    lo = (r1 - mid.astype(F32)).astype(BF16)
    s = _dot(tri, jnp.concatenate([hi, mid, lo], axis=1))
    return s[:, :LANE] + s[:, LANE:2 * LANE] + s[:, 2 * LANE:]


def _ssd_chunk(act_ref, dt_ref, y_ref, st_ref, ehead_ref, erep_ref, start, d, a_row, want_y):
    n = D_CHUNK
    rows = pl.ds(start, n)
    dtc = dt_ref[0, rows, :]
    r = lax.broadcasted_iota(jnp.int32, (n, n), 0)
    c = lax.broadcasted_iota(jnp.int32, (n, n), 1)
    causal = (r >= c) if d == 0 else (r <= c)
    acs = _chunk_cumsum(dtc * a_row, causal.astype(BF16))
    last = n - 1 if d == 0 else 0
    to_end = jnp.exp(acs[last:last + 1, :] - acs)
    spread = _select_cols(jnp.concatenate([dtc, jnp.exp(acs), to_end], axis=0), ehead_ref[d])
    dt_x, ea_x, te_x = spread[0:n], spread[n:2 * n], spread[2 * n:]
    xdt = act_ref[rows, 0:D_INNER] * dt_x
    xw = (xdt * te_x).astype(BF16)
    xdt = xdt.astype(BF16)
    gw = D_INNER // D_GROUPS
    heads_per_group = D_HEADS // D_GROUPS
    if want_y:
        acs_t = acs.T
        acs2 = jnp.concatenate(_split2(acs), axis=1)
        lower = lax.broadcasted_iota(jnp.int32, (n, LANE), 1) < D_HEAD_DIM
    y_blocks, carried = [], []
    for g in range(D_GROUPS):
        b0 = D_INNER + g * D_STATE
        c0 = D_INNER + D_GROUPS * D_STATE + g * D_STATE
        bg = act_ref[rows, b0:b0 + D_STATE].astype(BF16)
        st = st_ref[g]
        if want_y:
            cg = act_ref[rows, c0:c0 + D_STATE].astype(BF16)
            gram = _dot_nt(cg, bg)
            carried.append(_dot(cg, st.astype(BF16)))
            for pr in range(heads_per_group // 2):
                pair = g * (heads_per_group // 2) + pr
                pcol = d * (D_HEADS // 2) + pair
                rep = _dot(acs2, erep_ref[pcol])
                weights = []
                for e in range(2):
                    col = 2 * pcol + e
                    decay = jnp.where(causal, jnp.exp(rep[:, e * LANE:(e + 1) * LANE] - acs_t[col:col + 1, :]), 0.0)
                    weights.append((gram * decay).astype(BF16))
                yy = _dot(jnp.concatenate(weights, axis=0), xdt[:, pair * LANE:(pair + 1) * LANE])
                y_blocks.append(jnp.where(lower, yy[0:n], yy[n:]))
        upd = lax.dot_general(bg, xw[:, g * gw:(g + 1) * gw], (((0,), (0,)), ((), ())),
                              preferred_element_type=F32)
        st_ref[g] = ea_x[last:last + 1, g * gw:(g + 1) * gw] * st + upd
    if want_y:
        y = jnp.concatenate(y_blocks, axis=-1) + ea_x * jnp.concatenate(carried, axis=-1)
        y_ref[rows, :] = y_ref[rows, :] + y


def _ssd_body(need_ctx, *refs):
    if need_ctx:
        (xc_ref, dtc_ref, zc_ref, xx_ref, dtx_ref, zx_ref, cw_ref, cb_ref, alog_ref, dskip_ref, mn_ref,
         ehead_ref, erep_ref, oc_ref, ox_ref, actc_ref, actx_ref, yc_ref, yx_ref, st_ref) = refs
    else:
        (xc_ref, dtc_ref, xx_ref, dtx_ref, zx_ref, cw_ref, cb_ref, alog_ref, dskip_ref, mn_ref,
         ehead_ref, erep_ref, ox_ref, actc_ref, actx_ref, yx_ref, st_ref) = refs
        zc_ref = oc_ref = yc_ref = None
    for src, act in ((xc_ref, actc_ref), (xx_ref, actx_ref)):
        for blk in range(ODD_XBC // LANE):
            sl = slice(blk * LANE, (blk + 1) * LANE)
            v = _short_conv(src[0, :, sl].astype(F32), cw_ref.at[:, sl], cb_ref.at[:, sl])
            act[:, sl] = _silu(v)
    a_row = -jnp.exp(alog_ref[...])
    nc_c = actc_ref.shape[0] // D_CHUNK
    nc_x = actx_ref.shape[0] // D_CHUNK
    st_ref[...] = jnp.zeros_like(st_ref)
    yx_ref[...] = jnp.zeros_like(yx_ref)
    if need_ctx:
        yc_ref[...] = jnp.zeros_like(yc_ref)

    def ctx_step(k, carry):
        for d, kk in ((0, k), (1, nc_c - 1 - k)):
            _ssd_chunk(actc_ref, dtc_ref, yc_ref, st_ref.at[d], ehead_ref, erep_ref,
                       pl.multiple_of(kk * D_CHUNK, D_CHUNK), d, a_row, need_ctx)
        return carry

    def lat_step(k, carry):
        for d, kk in ((0, k), (1, nc_x - 1 - k)):
            _ssd_chunk(actx_ref, dtx_ref, yx_ref, st_ref.at[d], ehead_ref, erep_ref,
                       pl.multiple_of(kk * D_CHUNK, D_CHUNK), d, a_row, True)
        return carry

    lax.fori_loop(0, nc_c, ctx_step, 0)
    lax.fori_loop(0, nc_x, lat_step, 0, unroll=2)

    def finish(act, y_ref, z_ref, o_ref):
        def step(k, carry):
            rows = pl.ds(pl.multiple_of(k * D_CHUNK, D_CHUNK), D_CHUNK)
            y = y_ref[rows, :] + act[rows, 0:D_INNER] * dskip_ref[...]
            yz = y * _silu(z_ref[0, rows, :].astype(F32))
            o_ref[0, rows, :] = (_rms_rows(yz) * mn_ref[...]).astype(BF16)
            return carry
        lax.fori_loop(0, act.shape[0] // D_CHUNK, step, 0)

    finish(actx_ref, yx_ref, zx_ref, ox_ref)
    if need_ctx:
        finish(actc_ref, yc_ref, zc_ref, oc_ref)


def _ssd_selectors():
    lane = jnp.arange(LANE)[:, None]
    head = jnp.arange(D_INNER)[None, :] // D_HEAD_DIM
    ehead = jnp.stack([lane == d * D_HEADS + head for d in range(2)])
    pair_col = 2 * jnp.arange(ODD_DT // 2)[:, None, None] + jnp.arange(2 * LANE)[None, None, :] // LANE
    erep = lane[None] == pair_col
    twice = lambda sel: jnp.concatenate([sel, sel], axis=1).astype(BF16)
    return twice(ehead), twice(erep)


def _ssd(ctx_in, lat_in, conv_w, conv_b, alog, dskip, mnorm, need_ctx):
    xc, dtc, zc = ctx_in
    xx, dtx, zx = lat_in
    bsz, n, _ = xx.shape
    nc = xc.shape[1]
    blk = lambda a: pl.BlockSpec((1,) + a.shape[1:], lambda b: (b, 0, 0))
    seqs = [xc, dtc] + ([zc] if need_ctx else []) + [xx, dtx, zx]
    consts = [conv_w, conv_b, alog, dskip, mnorm, *_ssd_selectors()]
    out_x = jax.ShapeDtypeStruct((bsz, n, D_INNER), BF16)
    out_c = jax.ShapeDtypeStruct((bsz, nc, D_INNER), BF16)
    out_shape = [out_c, out_x] if need_ctx else [out_x]
    scratch = [pltpu.VMEM((nc, ODD_XBC), F32), pltpu.VMEM((n, ODD_XBC), F32)]
    scratch += ([pltpu.VMEM((nc, D_INNER), F32)] if need_ctx else []) + [pltpu.VMEM((n, D_INNER), F32)]
    scratch += [pltpu.VMEM((2, D_GROUPS, D_STATE, D_INNER // D_GROUPS), F32)]
    outs = pl.pallas_call(
        functools.partial(_ssd_body, need_ctx),
        grid=(bsz,),
        in_specs=[blk(a) for a in seqs] + [_const_spec(c.shape) for c in consts],
        out_specs=[blk(o) for o in out_shape],
        out_shape=out_shape,
        scratch_shapes=scratch,
        compiler_params=_params("parallel"),
        name="ssd",
    )(*seqs, *consts)
    return (outs[0], outs[1]) if need_ctx else (None, outs[0])


def _axial_rope(n_tokens, rot_dim):
    rows = n_tokens // GRID_W
    r_idx, c_idx = jnp.meshgrid(jnp.arange(rows), jnp.arange(GRID_W), indexing="ij")
    quarter = rot_dim // 4
    inv_freq = ROPE_THETA ** (-jnp.arange(quarter, dtype=F32) / quarter)
    ang = jnp.concatenate([r_idx.reshape(-1, 1).astype(F32) * inv_freq,
                           c_idx.reshape(-1, 1).astype(F32) * inv_freq], axis=-1)
    return jnp.cos(ang), jnp.sin(ang)


def _rope_tables(n_tokens, rotate):
    ones, zeros = jnp.ones((n_tokens, 64), F32), jnp.zeros((n_tokens, 64), F32)
    if rotate:
        ca, sa = _axial_rope(n_tokens, A_HEAD_DIM)
        cb, sb = _axial_rope(n_tokens, B_ROPE_DIM)
    else:
        ca, sa = jnp.ones((n_tokens, 32), F32), jnp.zeros((n_tokens, 32), F32)
        cb, sb = jnp.ones((n_tokens, 16), F32), jnp.zeros((n_tokens, 16), F32)
    z32 = zeros[:, :32]
    return (jnp.tile(jnp.concatenate([ca, ca], -1), (1, 2)),
            jnp.tile(jnp.concatenate([-sa, sa], -1), (1, 2)),
            jnp.concatenate([ones, cb, cb, z32], -1),
            jnp.concatenate([zeros, -sb, sb, z32], -1))


def _even_weights(w_in, a_qn, a_kn, b_qn, b_kvn, b_wq, b_wkv):
    d = w_in.shape[0]
    ak, av, bkva, bkr, aq, bqa = jnp.split(w_in, [128, 256, 384, 416, 928], axis=1)
    z64, z32 = jnp.zeros((d, 64), F32), jnp.zeros((d, 32), F32)
    half = B_ROPE_DIM // 2
    kr = jnp.concatenate([z64, bkr, z32], 1)
    krs = jnp.concatenate([z64, bkr[:, half:], bkr[:, :half], z32], 1)
    av0 = jnp.concatenate([av[:, :A_HEAD_DIM], z64], 1)
    av1 = jnp.concatenate([av[:, A_HEAD_DIM:], z64], 1)
    win = jnp.concatenate([aq, bqa, ak, av0, av1, bkva, kr, krs], 1).astype(BF16)
    wq = b_wq.reshape(B_Q_RANK, B_HEADS, B_NOPE_DIM + B_ROPE_DIM)
    nope, r1, r2 = wq[..., :B_NOPE_DIM], wq[..., B_NOPE_DIM:B_NOPE_DIM + half], wq[..., B_NOPE_DIM + half:]
    zq64, zq32 = jnp.zeros_like(nope), jnp.zeros((B_Q_RANK, B_HEADS, 32), F32)
    wq1 = jnp.concatenate([nope, r1, r2, zq32], -1).reshape(B_Q_RANK, B_HEADS * B_PAD_DIM)
    wq2 = jnp.concatenate([zq64, r2, r1, zq32], -1).reshape(B_Q_RANK, B_HEADS * B_PAD_DIM)
    wkv = b_wkv.reshape(B_KV_RANK, B_HEADS, B_NOPE_DIM + B_V_DIM)
    wk = jnp.concatenate([wkv[..., :B_NOPE_DIM], jnp.zeros((B_KV_RANK, B_HEADS, 64), F32)], -1)
    wk = wk.reshape(B_KV_RANK, B_HEADS * B_PAD_DIM)
    wv = jnp.concatenate([wkv[..., B_NOPE_DIM:], jnp.zeros((B_KV_RANK, B_HEADS, 64), F32)], -1)
    wv = wv.reshape(B_KV_RANK, B_HEADS * LANE)
    blockdiag = lambda n: (jnp.arange(n)[:, None] // A_HEAD_DIM == jnp.arange(n)[None, :] // A_HEAD_DIM)
    return dict(
        win=win,
        wq=jnp.concatenate([wq1, wq2], 1).astype(BF16),
        wkv=jnp.concatenate([wk, wv], 1).astype(BF16),
        gqa=jnp.tile(a_qn, A_HEADS).reshape(1, -1), gka=jnp.tile(a_kn, A_KV_HEADS).reshape(1, -1),
        gqb=b_qn.reshape(1, -1), gkvb=b_kvn.reshape(1, -1),
        bdq=(blockdiag(A_HEADS * A_HEAD_DIM) / A_HEAD_DIM).astype(BF16),
        bdk=(blockdiag(A_KV_HEADS * A_HEAD_DIM) / A_HEAD_DIM).astype(BF16))


def kernel(x, c, ctx, c_ctx, ada_w, ada_b, norm_g, ff_w1, ff_w2, at_w_in, at_w_out, ga_q_norm, ga_k_norm,
           ml_q_norm, ml_kv_norm, ml_w_q_up, ml_w_kv_up, lc_w_in, lc_w_out, hy_short_w, hy_short_b, hy_w1,
           hy_b1, hy_w2, hy_b2, hy_w3, hy_freq, hy_skip, mb_conv_w, mb_conv_b, mb_a_log, mb_dt_bias, mb_d_skip,
           mb_norm):
    bsz, n_lat, d = x.shape
    n_ctx = ctx.shape[1]
    depth = ada_w.shape[0]
    rows = -(-(bsz + 1) // 8) * 8
    cond = jnp.concatenate([c, c_ctx[None], jnp.zeros((rows - bsz - 1, d), F32)], axis=0)
    mods = _ada_table(cond, ada_w, ada_b)
    rope_x = _rope_tables(n_lat, True)
    rope_c = _rope_tables(n_ctx, False)
    tm = 512
    for i in range(depth):
        need_ctx = i < depth - 1
        j = i // 2
        g = norm_g[i]
        mod_x = mods[i, :bsz].reshape(bsz, 6, d)
        mod_c = mods[i, bsz:bsz + 1].reshape(1, 6, d)
        if i % 2 == 0:
            ew = _even_weights(at_w_in[j], ga_q_norm[j], ga_k_norm[j], ml_q_norm[j], ml_kv_norm[j],
                               ml_w_q_up[j], ml_w_kv_up[j])
            px = _even_prep(x, mod_x, g, ew, rope_x, tm)
            pc = _even_prep(ctx, mod_c, g, ew, rope_c, tm)
            o1x, o2x = _attention(px[0:4:3], [px[1:3] + px[4:6], pc[1:3] + pc[4:6]], 512)
            if need_ctx:
                o1c, o2c = _attention(pc[0:4:3], [pc[1:3] + pc[4:6]], 256)
            w_out = at_w_out[j].astype(BF16)
        else:
            pad = jnp.zeros((d, LANE - ODD_DT), F32)
            win = jnp.concatenate([lc_w_in[j], pad], axis=1).astype(BF16)
            dtb = jnp.pad(mb_dt_bias[j].reshape(-1), (0, LANE - ODD_DT)).reshape(1, LANE)
            hy_x, z_x, xbc_x, dt_x = _odd_prep(x, mod_x, g, win, dtb, tm)
            hy_c, z_c, xbc_c, dt_c = _odd_prep(ctx, mod_c, g, win, dtb, tm)
            filt = (hy_w1[j], hy_b1[j], hy_w2[j], hy_b2[j], hy_w3[j], hy_freq[j])
            sw, sb = hy_short_w[j], hy_short_b[j].reshape(1, -1)
            o1x = _hyena(hy_x, sw, sb, hy_skip[j], filt)
            alog = jnp.pad(mb_a_log[j].reshape(-1), (0, LANE - ODD_DT)).reshape(1, LANE)
            dskip = jnp.repeat(mb_d_skip[j], D_HEAD_DIM).reshape(1, D_INNER)
            o2c, o2x = _ssd((xbc_c, dt_c, z_c), (xbc_x, dt_x, z_x), mb_conv_w[j], mb_conv_b[j].reshape(1, -1),
                            alog, dskip, mb_norm[j].reshape(1, -1), need_ctx)
            if need_ctx:
                o1c = _hyena(hy_c, sw, sb, hy_skip[j], filt)
            w_out = lc_w_out[j].astype(BF16)
        w1, w2 = ff_w1[i].astype(BF16), ff_w2[i].astype(BF16)
        x = _post(x, o1x, o2x, mod_x, g, w_out[:MIX_HALF], w_out[MIX_HALF:], w1, w2, tm, i > 0)
        if need_ctx:
            ctx = _post(ctx, o1c, o2c, mod_c, g, w_out[:MIX_HALF], w_out[MIX_HALF:], w1, w2, tm, i > 0)
    return x
```

```python
import functools
import math

import jax
import jax.numpy as jnp
from jax import lax
from jax.experimental import pallas as pl
from jax.experimental.pallas import tpu as pltpu

F32 = jnp.float32
BF16 = jnp.bfloat16

D_MODEL = 1024
GRID_W = 64
EPS = 1e-6
ROPE_THETA = 10000.0
MIX_HALF = D_MODEL // 2
A_HEAD_DIM = 64
A_HEADS = MIX_HALF // A_HEAD_DIM
A_KV_HEADS = A_HEADS // 4
A_GROUP = A_HEADS // A_KV_HEADS
B_NOPE_DIM = 64
B_ROPE_DIM = 32
B_V_DIM = 64
B_HEADS = MIX_HALF // B_V_DIM
B_Q_RANK = D_MODEL // 4
B_KV_RANK = D_MODEL // 8
B_PAD_DIM = 128
C_WIDTH = MIX_HALF
C_ORDER = 2
C_POS_EMB = 33
C_FILTER_HIDDEN = 64
C_DECAY_TARGET = 1e-2
C_DECAY_FRAC_SHORT = 0.3
C_DECAY_FRAC_LONG = 1.5
D_INNER = MIX_HALF
D_HEAD_DIM = 64
D_HEADS = D_INNER // D_HEAD_DIM
D_GROUPS = 2
D_STATE = 128
D_CHUNK = 128
FF_HIDDEN = 4 * D_MODEL
FF_CHUNK = 1024
A_SCALE = A_HEAD_DIM ** -0.5
B_SCALE = (B_NOPE_DIM + B_ROPE_DIM) ** -0.5
LOG2E = math.log2(math.e)
V_DIM = 64
ODD_HY = (C_ORDER + 1) * C_WIDTH
ODD_Z = D_INNER
ODD_XBC = D_INNER + 2 * D_GROUPS * D_STATE
ODD_DT = 2 * D_HEADS
LANE = 128
VMEM_LIMIT = 56 * 2 ** 20


def _params(*sem):
    return pltpu.CompilerParams(dimension_semantics=sem, vmem_limit_bytes=VMEM_LIMIT)


def _const_spec(shape):
    nd = len(shape)
    return pl.BlockSpec(shape, lambda *_: (0,) * nd, pipeline_mode=pl.Buffered(1))


def _dot(a, b):
    return jnp.dot(a, b, preferred_element_type=F32)


def _dot_nt(a, b):
    return lax.dot_general(a, b, (((1,), (1,)), ((), ())), preferred_element_type=F32)


def _dot_f32(a, b):
    return jnp.dot(a, b, preferred_element_type=F32, precision=lax.Precision.HIGHEST)


def _rms_rows(x):
    return x * lax.rsqrt(jnp.mean(x * x, axis=-1, keepdims=True) + EPS)


def _silu(x):
    return x * jax.nn.sigmoid(x)


def _swap_halves(x, half):
    n = x.shape[-1]
    lane = lax.broadcasted_iota(jnp.int32, x.shape, 1)
    r1 = pltpu.roll(x, half, 1)
    r2 = pltpu.roll(x, n - half, 1)
    i1 = pltpu.roll(lane, half, 1)
    return jnp.where(i1 == (lane ^ half), r1, r2)


def _row_neighbours(u):
    n = u.shape[0]
    row = lax.broadcasted_iota(jnp.int32, u.shape, 0)
    prev = jnp.where(row == 0, 0.0, pltpu.roll(u, 1, 0))
    nxt = jnp.where(row == n - 1, 0.0, pltpu.roll(u, n - 1, 0))
    return prev, nxt


def _short_conv(u, w_ref, b_ref):
    prev, nxt = _row_neighbours(u)
    return w_ref[0:1, :] * prev + w_ref[1:2, :] * u + w_ref[2:3, :] * nxt + b_ref[...]


def _ada_body(cond_ref, w_ref, b_ref, o_ref):
    h = _silu(cond_ref[...]).astype(BF16)
    o_ref[0] = _dot(h, w_ref[0].astype(BF16)) + b_ref[0]


def _ada_table(cond, ada_w, ada_b):
    depth, d, n = ada_w.shape
    r = cond.shape[0]
    tn = 1536
    return pl.pallas_call(
        _ada_body,
        grid=(depth, n // tn),
        in_specs=[pl.BlockSpec((r, d), lambda l, j: (0, 0)),
                  pl.BlockSpec((1, d, tn), lambda l, j: (l, 0, j)),
                  pl.BlockSpec((1, 1, tn), lambda l, j: (l, 0, j))],
        out_specs=pl.BlockSpec((1, r, tn), lambda l, j: (l, 0, j)),
        out_shape=jax.ShapeDtypeStruct((depth, r, n), F32),
        compiler_params=_params("arbitrary", "arbitrary"),
        name="ada_table",
    )(cond, ada_w, ada_b.reshape(depth, 1, n))


def _mod_index(per_batch):
    return (lambda b, i: (b, 0, 0)) if per_batch else (lambda b, i: (0, 0, 0))


def _norm_mod(x, g_row, mod, shift_row, scale_row):
    h = _rms_rows(x) * g_row
    return h * (1.0 + mod[scale_row:scale_row + 1]) + mod[shift_row:shift_row + 1]


def _even_prep_body(s_ref, mod_ref, g_ref, win_ref, wq_ref, wkv_ref, gqa_ref, gka_ref, gqb_ref, gkvb_ref,
                    bdq_ref, bdk_ref, ca_ref, sa_ref, cb_ref, sb_ref,
                    qa_ref, ka_ref, va_ref, qb_ref, kb_ref, vb_ref):
    h = _norm_mod(s_ref[0], g_ref[0:1], mod_ref[0], 0, 1).astype(BF16)
    p = _dot(h, win_ref[...])
    aq, bqa = p[:, 0:512], p[:, 512:768]
    ak, av, bkva = p[:, 768:896], p[:, 896:1152], p[:, 1152:1280]
    kr, krs = p[:, 1280:1408], p[:, 1408:1536]
    ca, sa, cb, sb = ca_ref[...], sa_ref[...], cb_ref[...], sb_ref[...]
    upper = lax.broadcasted_iota(jnp.int32, (h.shape[0], LANE), 1) >= V_DIM

    aqn = aq * lax.rsqrt(_dot((aq * aq).astype(BF16), bdq_ref[...]) + EPS) * gqa_ref[...]
    aqs = _swap_halves(aqn, A_HEAD_DIM // 2)
    for blk in range(4):
        sl = slice(blk * LANE, (blk + 1) * LANE)
        q2 = ((aqn[:, sl] * ca + aqs[:, sl] * sa) * (A_SCALE * LOG2E)).astype(BF16)
        qa_ref[0, 2 * blk] = q2[:, :A_HEAD_DIM]
        qa_ref[0, 2 * blk + 1] = q2[:, A_HEAD_DIM:]
    akn = ak * lax.rsqrt(_dot((ak * ak).astype(BF16), bdk_ref[...]) + EPS) * gka_ref[...]
    akr = (akn * ca + _swap_halves(akn, A_HEAD_DIM // 2) * sa).astype(BF16)
    for j in range(A_KV_HEADS):
        ka_ref[0, j] = akr[:, j * A_HEAD_DIM:(j + 1) * A_HEAD_DIM]
        va_ref[0, j] = jnp.where(upper, 1.0, av[:, j * LANE:(j + 1) * LANE]).astype(BF16)

    bqn = (_rms_rows(bqa) * gqb_ref[...]).astype(BF16)
    nq = B_HEADS * B_PAD_DIM
    u = _dot(bqn, wq_ref[:, :nq])
    us = _dot(bqn, wq_ref[:, nq:])
    bkvn = (_rms_rows(bkva) * gkvb_ref[...]).astype(BF16)
    uk = _dot(bkvn, wkv_ref[:, :nq])
    uv = _dot(bkvn, wkv_ref[:, nq:])
    krr = kr * cb + krs * sb
    for hd in range(B_HEADS):
        sl = slice(hd * B_PAD_DIM, (hd + 1) * B_PAD_DIM)
        qb_ref[0, hd] = ((u[:, sl] * cb + us[:, sl] * sb) * (B_SCALE * LOG2E)).astype(BF16)
        kb_ref[0, hd] = (uk[:, sl] + krr).astype(BF16)
        vb_ref[0, hd] = jnp.where(upper, 1.0, uv[:, sl]).astype(BF16)


def _even_prep(s, mod, g, ew, rope, tm):
    bsz, t, d = s.shape
    tm = min(tm, t)
    per_batch = mod.shape[0] != 1
    row = lambda b, i: (b, i, 0)
    hrow = lambda b, i: (b, 0, i, 0)
    tab = lambda b, i: (i, 0)
    consts = [ew["win"], ew["wq"], ew["wkv"], ew["gqa"], ew["gka"], ew["gqb"], ew["gkvb"], ew["bdq"], ew["bdk"]]
    outs = [(A_HEADS, A_HEAD_DIM), (A_KV_HEADS, A_HEAD_DIM), (A_KV_HEADS, LANE),
            (B_HEADS, B_PAD_DIM), (B_HEADS, B_PAD_DIM), (B_HEADS, LANE)]
    return pl.pallas_call(
        _even_prep_body,
        grid=(bsz, t // tm),
        in_specs=[pl.BlockSpec((1, tm, d), row),
                  pl.BlockSpec((1, 6, d), _mod_index(per_batch)),
                  _const_spec(g.shape)]
        + [_const_spec(c.shape) for c in consts]
        + [pl.BlockSpec((tm, LANE), tab)] * 4,
        out_specs=[pl.BlockSpec((1, nh, tm, hd), hrow) for nh, hd in outs],
        out_shape=[jax.ShapeDtypeStruct((bsz, nh, t, hd), BF16) for nh, hd in outs],
        compiler_params=_params("parallel", "arbitrary"),
        name="even_prep",
    )(s, mod, g, *consts, *rope)


def _softmax_attend(q, kvs):
    ss = [_dot_nt(q, k) for k, _ in kvs]
    m = ss[0].max(axis=-1, keepdims=True)
    for s in ss[1:]:
        m = jnp.maximum(m, s.max(axis=-1, keepdims=True))
    acc = None
    for s, (_, v) in zip(ss, kvs):
        o = _dot(jnp.exp2(s - m).astype(BF16), v)
        acc = o if acc is None else acc + o
    return acc / pltpu.roll(acc, V_DIM, 1)


def _attn_body(n_src, qa_ref, qb_ref, *refs):
    kv_refs = refs[:4 * n_src]
    oa_ref, ob_ref = refs[4 * n_src:]
    srcs = [kv_refs[4 * i:4 * i + 4] for i in range(n_src)]
    lower = lax.broadcasted_iota(jnp.int32, (qa_ref.shape[2], LANE), 1) < V_DIM

    def pair_out(first, second):
        return jnp.where(lower, first, pltpu.roll(second, V_DIM, 1)).astype(BF16)

    for pair in range(A_HEADS // 2):
        outs = []
        for hd in (2 * pair, 2 * pair + 1):
            j = hd // A_GROUP
            outs.append(_softmax_attend(qa_ref[0, hd], [(ka[0, j], va[0, j]) for ka, va, _, _ in srcs]))
        oa_ref[0, :, pair * LANE:(pair + 1) * LANE] = pair_out(*outs)
    for pair in range(B_HEADS // 2):
        outs = []
        for hd in (2 * pair, 2 * pair + 1):
            outs.append(_softmax_attend(qb_ref[0, hd], [(kb[0, hd], vb[0, hd]) for _, _, kb, vb in srcs]))
        ob_ref[0, :, pair * LANE:(pair + 1) * LANE] = pair_out(*outs)


def _attention(q, kv_srcs, tq):
    qa, qb = q
    bsz, _, t, _ = qa.shape
    tq = min(tq, t)
    qrow = lambda b, i: (b, 0, i, 0)
    whole = lambda b, i: (b, 0, 0, 0)
    in_specs = [pl.BlockSpec((1, A_HEADS, tq, A_HEAD_DIM), qrow), pl.BlockSpec((1, B_HEADS, tq, B_PAD_DIM), qrow)]
    args = [qa, qb]
    for src in kv_srcs:
        for a in src:
            in_specs.append(pl.BlockSpec((1,) + a.shape[1:], whole))
            args.append(a)
    out = jax.ShapeDtypeStruct((bsz, t, MIX_HALF), BF16)
    return pl.pallas_call(
        functools.partial(_attn_body, len(kv_srcs)),
        grid=(bsz, t // tq),
        in_specs=in_specs,
        out_specs=[pl.BlockSpec((1, tq, MIX_HALF), lambda b, i: (b, i, 0))] * 2,
        out_shape=[out, out],
        compiler_params=_params("parallel", "arbitrary"),
        name="attention",
    )(*args)


def _post_body(s_ref, o1_ref, o2_ref, mod_ref, g_ref, wo1_ref, wo2_ref, w1_ref, w2_ref, out_ref):
    mod = mod_ref[0]
    mo = _dot(o1_ref[0], wo1_ref[...]) + _dot(o2_ref[0], wo2_ref[...])
    x1 = s_ref[0] + mod[2:3] * (_rms_rows(mo) * g_ref[1:2])
    h = _norm_mod(x1, g_ref[2:3], mod, 3, 4).astype(BF16)
    acc = None
    for c in range(FF_HIDDEN // FF_CHUNK):
        sl = slice(c * FF_CHUNK, (c + 1) * FF_CHUNK)
        u = jnp.square(jnp.maximum(_dot(h, w1_ref[:, sl]), 0.0)).astype(BF16)
        part = _dot(u, w2_ref[sl, :])
        acc = part if acc is None else acc + part
    out_ref[0] = x1 + mod[5:6] * (_rms_rows(acc) * g_ref[3:4])


def _post(s, o1, o2, mod, g, wo1, wo2, w1, w2, tm, in_place):
    bsz, t, d = s.shape
    tm = min(tm, t)
    per_batch = mod.shape[0] != 1
    row = lambda b, i: (b, i, 0)
    return pl.pallas_call(
        _post_body,
        grid=(bsz, t // tm),
        in_specs=[pl.BlockSpec((1, tm, d), row),
                  pl.BlockSpec((1, tm, MIX_HALF), row),
                  pl.BlockSpec((1, tm, MIX_HALF), row),
                  pl.BlockSpec((1, 6, d), _mod_index(per_batch)),
                  _const_spec(g.shape), _const_spec(wo1.shape), _const_spec(wo2.shape),
                  _const_spec(w1.shape), _const_spec(w2.shape)],
        out_specs=pl.BlockSpec((1, tm, d), row),
        out_shape=jax.ShapeDtypeStruct(s.shape, F32),
        input_output_aliases={0: 0} if in_place else {},
        compiler_params=_params("parallel", "arbitrary"),
        name="post",
    )(s, o1, o2, mod, g, wo1, wo2, w1, w2)


ODD_NIN = ODD_HY + ODD_Z + ODD_XBC + LANE


def _softplus(x):
    return jnp.maximum(x, 0.0) + jnp.log1p(jnp.exp(-jnp.abs(x)))


def _odd_prep_body(s_ref, mod_ref, g_ref, win_ref, dtb_ref, hy_ref, z_ref, xbc_ref, dt_ref):
    h = _norm_mod(s_ref[0], g_ref[0:1], mod_ref[0], 0, 1).astype(BF16)
    p = _dot(h, win_ref[...])
    hy_ref[0] = p[:, :ODD_HY].astype(BF16)
    z_ref[0] = p[:, ODD_HY:ODD_HY + ODD_Z].astype(BF16)
    xbc_ref[0] = p[:, ODD_HY + ODD_Z:ODD_HY + ODD_Z + ODD_XBC].astype(BF16)
    dt_ref[0] = _softplus(p[:, ODD_HY + ODD_Z + ODD_XBC:] + dtb_ref[...])


def _odd_prep(s, mod, g, win, dtb, tm):
    bsz, t, d = s.shape
    tm = min(tm, t)
    per_batch = mod.shape[0] != 1
    row = lambda b, i: (b, i, 0)
    widths = [(ODD_HY, BF16), (ODD_Z, BF16), (ODD_XBC, BF16), (LANE, F32)]
    return pl.pallas_call(
        _odd_prep_body,
        grid=(bsz, t // tm),
        in_specs=[pl.BlockSpec((1, tm, d), row),
                  pl.BlockSpec((1, 6, d), _mod_index(per_batch)),
                  _const_spec(g.shape), _const_spec(win.shape), _const_spec(dtb.shape)],
        out_specs=[pl.BlockSpec((1, tm, w), row) for w, _ in widths],
        out_shape=[jax.ShapeDtypeStruct((bsz, t, w), dt) for w, dt in widths],
        compiler_params=_params("parallel", "arbitrary"),
        name="odd_prep",
    )(s, mod, g, win, dtb)


def _filter_body(n, z_ref, w1_ref, b1_ref, w2_ref, b2_ref, w3_ref, fr_ref, dec_ref, ts_ref, rs_ref):
    fr = fr_ref[...]
    h = jnp.sin(fr * (_dot_f32(z_ref[...], w1_ref[...]) + b1_ref[...]))
    h = jnp.sin(fr * (_dot_f32(h, w2_ref[...]) + b2_ref[...]))
    taps = _dot_f32(h, w3_ref[0, 0])
    tn = dec_ref.shape[1]
    fwd, bwd = taps[:, :tn] * dec_ref[...], taps[:, tn:] * dec_ref[...]
    lag = lax.broadcasted_iota(jnp.int32, fwd.shape, 0) + (pl.program_id(0) * fwd.shape[0] - n)
    ts_ref[0] = jnp.where(lag == -n, 0.0, jnp.where(lag >= 0, fwd, bwd))
    rs_ref[0] = jnp.where(lag == -n, 0.0, jnp.where(lag > 0, bwd, fwd))


def _hyena_taps(n, w1, b1, w2, b2, w3, freq):
    pos = jnp.minimum(jnp.abs(jnp.arange(2 * n) - n), n - 1)
    t = jnp.linspace(0.0, 1.0, n, dtype=F32)[pos]
    bands = (C_POS_EMB - 1) // 2
    w = (2.0 * math.pi * jnp.arange(n, dtype=F32) / n)[pos]
    fb = jnp.linspace(1e-4, bands - 1, bands, dtype=F32)
    ph = w[:, None] * fb[None, :]
    z = jnp.concatenate([t[:, None], jnp.cos(ph), -jnp.sin(ph)], axis=-1)
    decay_max = math.log(C_DECAY_TARGET) / C_DECAY_FRAC_SHORT
    decay_min = math.log(C_DECAY_TARGET) / C_DECAY_FRAC_LONG
    deltas = jnp.abs(jnp.linspace(decay_min, decay_max, C_WIDTH, dtype=F32))
    dec = jnp.exp(-t[:, None] * deltas[None, :])
    pad_h = LANE - C_FILTER_HIDDEN
    zp = jnp.pad(z, ((0, 0), (0, LANE - C_POS_EMB)))
    w1p = jnp.pad(w1, ((0, LANE - C_POS_EMB), (0, pad_h)))
    w2p = jnp.pad(w2, ((0, pad_h), (0, pad_h)))
    tn = 256
    nbc = C_WIDTH // tn
    w3p = jnp.pad(w3, ((0, pad_h), (0, 0))).reshape(LANE, C_ORDER, 2, nbc, tn)
    w3p = w3p.transpose(1, 3, 0, 2, 4).reshape(C_ORDER, nbc, LANE, 2 * tn)
    row = lambda v: jnp.pad(v, (0, pad_h)).reshape(1, LANE)
    tr = min(2 * n, 512)
    const = lambda shape: pl.BlockSpec(shape, lambda i, o, j: (0,) * len(shape))
    out = jax.ShapeDtypeStruct((C_ORDER, 2 * n, C_WIDTH), F32)
    return pl.pallas_call(
        functools.partial(_filter_body, n),
        grid=(2 * n // tr, C_ORDER, nbc),
        in_specs=[pl.BlockSpec((tr, LANE), lambda i, o, j: (i, 0)),
                  const((LANE, LANE)), const((1, LANE)), const((LANE, LANE)), const((1, LANE)),
                  pl.BlockSpec((1, 1, LANE, 2 * tn), lambda i, o, j: (o, j, 0, 0)),
                  const((1, LANE)),
                  pl.BlockSpec((tr, tn), lambda i, o, j: (i, j))],
        out_specs=[pl.BlockSpec((1, tr, tn), lambda i, o, j: (o, i, j))] * 2,
        out_shape=[out, out],
        compiler_params=_params("arbitrary", "arbitrary", "arbitrary"),
        name="hyena_taps",
    )(zp, w1p, row(b1), w2p, row(b2), w3p, row(freq), dec)


def _dft_matrices(n):
    m = 2 * n
    k = jnp.arange(n, dtype=jnp.int32)[:, None]
    s = jnp.arange(n, dtype=jnp.int32)[None, :]
    ang = ((k * s) % m).astype(F32) * (2.0 * math.pi / m)
    nyq = jnp.where(s % 2 == 0, 1.0, -1.0).astype(F32)
    cosm = jnp.cos(ang)
    sinm = jnp.where(k == 0, nyq, -jnp.sin(ang))
    fwd = jnp.concatenate([cosm, sinm], axis=0)
    wgt = jnp.where(k == 0, 1.0 / m, 2.0 / m)
    inv = jnp.concatenate([(cosm * wgt).T, (sinm * wgt).T], axis=1)
    return fwd.astype(BF16), inv.astype(BF16)


DFT_ROWS = 256


def _first_row(shape, offset=0):
    return lax.broadcasted_iota(jnp.int32, shape, 0) + offset == 0


def _dft_row_tiles(n, step):
    tr = min(DFT_ROWS, n)

    def body(r, carry):
        r0 = pl.multiple_of(r * tr, tr)
        step(r0, pl.ds(r0, tr), pl.ds(pl.multiple_of(n + r0, tr), tr))
        return carry
    lax.fori_loop(0, n // tr, body, 0)


def _spectrum_body(n, f_ref, fw_ref, bw_ref, o_ref, fs_ref, bs_ref):
    fw = fw_ref[0]
    bw = jnp.where(_first_row(fw.shape), 0.0, bw_ref[0])
    for u, s_ref in ((fw, fs_ref), (bw, bs_ref)):
        hi = u.astype(BF16)
        s_ref[0] = hi
        s_ref[1] = (u - hi.astype(F32)).astype(BF16)

    def step(r0, rows, rows_im):
        def packed(s_ref, rr):
            return _dot(f_ref[rr, :], s_ref[0]) + _dot(f_ref[rr, :], s_ref[1])
        o_ref[0, 0, rows, :] = packed(fs_ref, rows) + packed(bs_ref, rows)
        im1, im2 = packed(fs_ref, rows_im), packed(bs_ref, rows_im)
        o_ref[0, 0, rows_im, :] = jnp.where(_first_row(im1.shape, r0), im1 + im2, im1 - im2)
    _dft_row_tiles(n, step)


def _hyena_spectrum(ts, rs, fwd, tb, cw):
    n = ts.shape[1] // 2
    nb = n // tb
    nd = 2 * nb - 1
    return pl.pallas_call(
        functools.partial(_spectrum_body, tb),
        grid=(C_ORDER, nd, C_WIDTH // cw),
        in_specs=[_const_spec(fwd.shape),
                  pl.BlockSpec((1, tb, cw), lambda o, di, j: (o, di + 1, j)),
                  pl.BlockSpec((1, tb, cw), lambda o, di, j: (o, 2 * nb - 1 - di, j))],
        out_specs=pl.BlockSpec((1, 1, 2 * tb, cw), lambda o, di, j: (o, di, 0, j)),
        out_shape=jax.ShapeDtypeStruct((C_ORDER, nd, 2 * tb, C_WIDTH), F32),
        scratch_shapes=[pltpu.VMEM((2, tb, cw), BF16), pltpu.VMEM((2, tb, cw), BF16)],
        compiler_params=_params("arbitrary", "arbitrary", "arbitrary"),
        name="hyena_spectrum",
    )(fwd, ts, rs)


HY_BLOCK = 512
HY_TILE = 64


def _hy_conv_body(n, tb, prev_conv, prev_ref, pw_ref, pb_ref, gate_ref, gw_ref, gb_ref, skip_ref, f_ref, g_ref,
                  h_ref, o_ref, zb_ref, u_ref, yb_ref, gate_s, skip_s):
    nb = n // tb
    prev = prev_ref[0].astype(F32)
    if prev_conv:
        prev = _short_conv(prev, pw_ref, pb_ref)
    zb_ref[...] = prev.astype(BF16)
    gate = _short_conv(gate_ref[0].astype(F32), gw_ref, gb_ref)
    gate_s[...] = gate
    skip_s[...] = gate * (prev * skip_ref[0])
    for j in range(nb):
        u_ref[j] = _dot(f_ref[...], zb_ref[j * tb:(j + 1) * tb, :])

    def mac_tile(i, r0):
        rows, rows_im = pl.ds(r0, HY_TILE), pl.ds(tb + r0, HY_TILE)
        re_acc = im_acc = None
        for j in range(nb):
            di = i - j + nb - 1
            ur, ui = u_ref[j, rows, :], u_ref[j, rows_im, :]
            hr, hi = h_ref[0, di, rows, :], h_ref[0, di, rows_im, :]
            imim = ui * hi
            if r0 == 0:
                first = _first_row(ur.shape)
                re_t = ur * hr - jnp.where(first, 0.0, imim)
                im_t = jnp.where(first, imim, ur * hi + ui * hr)
            else:
                re_t = ur * hr - imim
                im_t = ur * hi + ui * hr
            re_acc = re_t if re_acc is None else re_acc + re_t
            im_acc = im_t if im_acc is None else im_acc + im_t
        yb_ref[i, rows, :] = re_acc.astype(BF16)
        yb_ref[i, rows_im, :] = im_acc.astype(BF16)

    for i in range(nb):
        for r in range(tb // HY_TILE):
            mac_tile(i, r * HY_TILE)
        y = _dot(g_ref[...], yb_ref[i])
        sl = slice(i * tb, (i + 1) * tb)
        o_ref[0, sl, :] = (gate_s[sl, :] * y + skip_s[sl, :]).astype(BF16)


def _hy_conv(prev, prev_part, prev_conv, hy, gate_part, conv_w, conv_b, skip, fwd, inv, spec, order, tb, cw):
    bsz, n, _ = hy.shape
    nb = C_WIDTH // cw
    nd = spec.shape[1]
    return pl.pallas_call(
        functools.partial(_hy_conv_body, n, tb, prev_conv),
        grid=(nb, bsz),
        in_specs=[pl.BlockSpec((1, n, cw), lambda j, b: (b, 0, prev_part * nb + j)),
                  pl.BlockSpec((3, cw), lambda j, b: (0, prev_part * nb + j)),
                  pl.BlockSpec((1, cw), lambda j, b: (0, prev_part * nb + j)),
                  pl.BlockSpec((1, n, cw), lambda j, b: (b, 0, gate_part * nb + j)),
                  pl.BlockSpec((3, cw), lambda j, b: (0, gate_part * nb + j)),
                  pl.BlockSpec((1, cw), lambda j, b: (0, gate_part * nb + j)),
                  pl.BlockSpec((1, 1, cw), lambda j, b: (order, 0, j)),
                  _const_spec(fwd.shape), _const_spec(inv.shape),
                  pl.BlockSpec((1, nd, 2 * tb, cw), lambda j, b: (order, 0, 0, j))],
        out_specs=pl.BlockSpec((1, n, cw), lambda j, b: (b, 0, j)),
        out_shape=jax.ShapeDtypeStruct((bsz, n, C_WIDTH), BF16),
        scratch_shapes=[pltpu.VMEM((n, cw), BF16), pltpu.VMEM((n // tb, 2 * tb, cw), F32),
                        pltpu.VMEM((n // tb, 2 * tb, cw), BF16), pltpu.VMEM((n, cw), F32), pltpu.VMEM((n, cw), F32)],
        compiler_params=_params("parallel", "arbitrary"),
        name="hyena_conv",
    )(prev, conv_w, conv_b, hy, conv_w, conv_b, skip, fwd, inv, spec)


def _hyena(hy, conv_w, conv_b, skip, filt):
    n = hy.shape[1]
    tb = min(HY_BLOCK, n)
    cw = 256
    fwd, inv = _dft_matrices(tb)
    spec = _hyena_spectrum(*_hyena_taps(n, *filt), fwd, tb, cw)
    skip3 = skip.reshape(C_ORDER, 1, C_WIDTH)
    z1 = _hy_conv(hy, 0, True, hy, 1, conv_w, conv_b, skip3, fwd, inv, spec, 0, tb, cw)
    return _hy_conv(z1, 0, False, hy, 2, conv_w, conv_b, skip3, fwd, inv, spec, 1, tb, cw)


def _split2(x):
    hi = x.astype(BF16)
    return hi, (x - hi.astype(F32)).astype(BF16)


def _select_cols(x, sel2):
    return _dot(jnp.concatenate(_split2(x), axis=1), sel2)


def _chunk_cumsum(adt, tri):
    hi = adt.astype(BF16)
    r1 = adt - hi.astype(F32)
    mid = r1.astype(BF16)
    lo = (r1 - mid.astype(F32)).astype(BF16)
    s = _dot(tri, jnp.concatenate([hi, mid, lo], axis=1))
    return s[:, :LANE] + s[:, LANE:2 * LANE] + s[:, 2 * LANE:]


def _ssd_chunk(act_ref, dt_ref, y_ref, st_ref, ehead_ref, erep_ref, start, d, a_row, want_y):
    n = D_CHUNK
    rows = pl.ds(start, n)
    dtc = dt_ref[0, rows, :]
    r = lax.broadcasted_iota(jnp.int32, (n, n), 0)
    c = lax.broadcasted_iota(jnp.int32, (n, n), 1)
    causal = (r >= c) if d == 0 else (r <= c)
    acs = _chunk_cumsum(dtc * a_row, causal.astype(BF16))
    last = n - 1 if d == 0 else 0
    to_end = jnp.exp(acs[last:last + 1, :] - acs)
    spread = _select_cols(jnp.concatenate([dtc, jnp.exp(acs), to_end], axis=0), ehead_ref[d])
    dt_x, ea_x, te_x = spread[0:n], spread[n:2 * n], spread[2 * n:]
    xdt = act_ref[rows, 0:D_INNER] * dt_x
    xw = (xdt * te_x).astype(BF16)
    xdt = xdt.astype(BF16)
    gw = D_INNER // D_GROUPS
    heads_per_group = D_HEADS // D_GROUPS
    if want_y:
        acs_t = acs.T
        acs2 = jnp.concatenate(_split2(acs), axis=1)
        lower = lax.broadcasted_iota(jnp.int32, (n, LANE), 1) < D_HEAD_DIM
    y_blocks, carried = [], []
    for g in range(D_GROUPS):
        b0 = D_INNER + g * D_STATE
        c0 = D_INNER + D_GROUPS * D_STATE + g * D_STATE
        bg = act_ref[rows, b0:b0 + D_STATE].astype(BF16)
        st = st_ref[g]
        if want_y:
            cg = act_ref[rows, c0:c0 + D_STATE].astype(BF16)
            gram = _dot_nt(cg, bg)
            carried.append(_dot(cg, st.astype(BF16)))
            for pr in range(heads_per_group // 2):
                pair = g * (heads_per_group // 2) + pr
                pcol = d * (D_HEADS // 2) + pair
                rep = _dot(acs2, erep_ref[pcol])
                weights = []
                for e in range(2):
                    col = 2 * pcol + e
                    decay = jnp.where(causal, jnp.exp(rep[:, e * LANE:(e + 1) * LANE] - acs_t[col:col + 1, :]), 0.0)
                    weights.append((gram * decay).astype(BF16))
                yy = _dot(jnp.concatenate(weights, axis=0), xdt[:, pair * LANE:(pair + 1) * LANE])
                y_blocks.append(jnp.where(lower, yy[0:n], yy[n:]))
        upd = lax.dot_general(bg, xw[:, g * gw:(g + 1) * gw], (((0,), (0,)), ((), ())),
                              preferred_element_type=F32)
        st_ref[g] = ea_x[last:last + 1, g * gw:(g + 1) * gw] * st + upd
    if want_y:
        y = jnp.concatenate(y_blocks, axis=-1) + ea_x * jnp.concatenate(carried, axis=-1)
        y_ref[rows, :] = y_ref[rows, :] + y


def _ssd_body(need_ctx, *refs):
    if need_ctx:
        (xc_ref, dtc_ref, zc_ref, xx_ref, dtx_ref, zx_ref, cw_ref, cb_ref, alog_ref, dskip_ref, mn_ref,
         ehead_ref, erep_ref, oc_ref, ox_ref, actc_ref, actx_ref, yc_ref, yx_ref, st_ref) = refs
    else:
        (xc_ref, dtc_ref, xx_ref, dtx_ref, zx_ref, cw_ref, cb_ref, alog_ref, dskip_ref, mn_ref,
         ehead_ref, erep_ref, ox_ref, actc_ref, actx_ref, yx_ref, st_ref) = refs
        zc_ref = oc_ref = yc_ref = None
    for src, act in ((xc_ref, actc_ref), (xx_ref, actx_ref)):
        for blk in range(ODD_XBC // LANE):
            sl = slice(blk * LANE, (blk + 1) * LANE)
            v = _short_conv(src[0, :, sl].astype(F32), cw_ref.at[:, sl], cb_ref.at[:, sl])
            act[:, sl] = _silu(v)
    a_row = -jnp.exp(alog_ref[...])
    nc_c = actc_ref.shape[0] // D_CHUNK
    nc_x = actx_ref.shape[0] // D_CHUNK
    st_ref[...] = jnp.zeros_like(st_ref)
    yx_ref[...] = jnp.zeros_like(yx_ref)
    if need_ctx:
        yc_ref[...] = jnp.zeros_like(yc_ref)

    def ctx_step(k, carry):
        for d, kk in ((0, k), (1, nc_c - 1 - k)):
            _ssd_chunk(actc_ref, dtc_ref, yc_ref, st_ref.at[d], ehead_ref, erep_ref,
                       pl.multiple_of(kk * D_CHUNK, D_CHUNK), d, a_row, need_ctx)
        return carry

    def lat_step(k, carry):
        for d, kk in ((0, k), (1, nc_x - 1 - k)):
            _ssd_chunk(actx_ref, dtx_ref, yx_ref, st_ref.at[d], ehead_ref, erep_ref,
                       pl.multiple_of(kk * D_CHUNK, D_CHUNK), d, a_row, True)
        return carry

    lax.fori_loop(0, nc_c, ctx_step, 0)
    lax.fori_loop(0, nc_x, lat_step, 0, unroll=2)

    def finish(act, y_ref, z_ref, o_ref):
        def step(k, carry):
            rows = pl.ds(pl.multiple_of(k * D_CHUNK, D_CHUNK), D_CHUNK)
            y = y_ref[rows, :] + act[rows, 0:D_INNER] * dskip_ref[...]
            yz = y * _silu(z_ref[0, rows, :].astype(F32))
            o_ref[0, rows, :] = (_rms_rows(yz) * mn_ref[...]).astype(BF16)
            return carry
        lax.fori_loop(0, act.shape[0] // D_CHUNK, step, 0)

    finish(actx_ref, yx_ref, zx_ref, ox_ref)
    if need_ctx:
        finish(actc_ref, yc_ref, zc_ref, oc_ref)


def _ssd_selectors():
    lane = jnp.arange(LANE)[:, None]
    head = jnp.arange(D_INNER)[None, :] // D_HEAD_DIM
    ehead = jnp.stack([lane == d * D_HEADS + head for d in range(2)])
    pair_col = 2 * jnp.arange(ODD_DT // 2)[:, None, None] + jnp.arange(2 * LANE)[None, None, :] // LANE
    erep = lane[None] == pair_col
    twice = lambda sel: jnp.concatenate([sel, sel], axis=1).astype(BF16)
    return twice(ehead), twice(erep)


def _ssd(ctx_in, lat_in, conv_w, conv_b, alog, dskip, mnorm, need_ctx):
    xc, dtc, zc = ctx_in
    xx, dtx, zx = lat_in
    bsz, n, _ = xx.shape
    nc = xc.shape[1]
    blk = lambda a: pl.BlockSpec((1,) + a.shape[1:], lambda b: (b, 0, 0))
    seqs = [xc, dtc] + ([zc] if need_ctx else []) + [xx, dtx, zx]
    consts = [conv_w, conv_b, alog, dskip, mnorm, *_ssd_selectors()]
    out_x = jax.ShapeDtypeStruct((bsz, n, D_INNER), BF16)
    out_c = jax.ShapeDtypeStruct((bsz, nc, D_INNER), BF16)
    out_shape = [out_c, out_x] if need_ctx else [out_x]
    scratch = [pltpu.VMEM((nc, ODD_XBC), F32), pltpu.VMEM((n, ODD_XBC), F32)]
    scratch += ([pltpu.VMEM((nc, D_INNER), F32)] if need_ctx else []) + [pltpu.VMEM((n, D_INNER), F32)]
    scratch += [pltpu.VMEM((2, D_GROUPS, D_STATE, D_INNER // D_GROUPS), F32)]
    outs = pl.pallas_call(
        functools.partial(_ssd_body, need_ctx),
        grid=(bsz,),
        in_specs=[blk(a) for a in seqs] + [_const_spec(c.shape) for c in consts],
        out_specs=[blk(o) for o in out_shape],
        out_shape=out_shape,
        scratch_shapes=scratch,
        compiler_params=_params("parallel"),
        name="ssd",
    )(*seqs, *consts)
    return (outs[0], outs[1]) if need_ctx else (None, outs[0])


def _axial_rope(n_tokens, rot_dim):
    rows = n_tokens // GRID_W
    r_idx, c_idx = jnp.meshgrid(jnp.arange(rows), jnp.arange(GRID_W), indexing="ij")
    quarter = rot_dim // 4
    inv_freq = ROPE_THETA ** (-jnp.arange(quarter, dtype=F32) / quarter)
    ang = jnp.concatenate([r_idx.reshape(-1, 1).astype(F32) * inv_freq,
                           c_idx.reshape(-1, 1).astype(F32) * inv_freq], axis=-1)
    return jnp.cos(ang), jnp.sin(ang)


def _rope_tables(n_tokens, rotate):
    ones, zeros = jnp.ones((n_tokens, 64), F32), jnp.zeros((n_tokens, 64), F32)
    if rotate:
        ca, sa = _axial_rope(n_tokens, A_HEAD_DIM)
        cb, sb = _axial_rope(n_tokens, B_ROPE_DIM)
    else:
        ca, sa = jnp.ones((n_tokens, 32), F32), jnp.zeros((n_tokens, 32), F32)
        cb, sb = jnp.ones((n_tokens, 16), F32), jnp.zeros((n_tokens, 16), F32)
    z32 = zeros[:, :32]
    return (jnp.tile(jnp.concatenate([ca, ca], -1), (1, 2)),
            jnp.tile(jnp.concatenate([-sa, sa], -1), (1, 2)),
            jnp.concatenate([ones, cb, cb, z32], -1),
            jnp.concatenate([zeros, -sb, sb, z32], -1))


def _even_weights(w_in, a_qn, a_kn, b_qn, b_kvn, b_wq, b_wkv):
    d = w_in.shape[0]
    ak, av, bkva, bkr, aq, bqa = jnp.split(w_in, [128, 256, 384, 416, 928], axis=1)
    z64, z32 = jnp.zeros((d, 64), F32), jnp.zeros((d, 32), F32)
    half = B_ROPE_DIM // 2
    kr = jnp.concatenate([z64, bkr, z32], 1)
    krs = jnp.concatenate([z64, bkr[:, half:], bkr[:, :half], z32], 1)
    av0 = jnp.concatenate([av[:, :A_HEAD_DIM], z64], 1)
    av1 = jnp.concatenate([av[:, A_HEAD_DIM:], z64], 1)
    win = jnp.concatenate([aq, bqa, ak, av0, av1, bkva, kr, krs], 1).astype(BF16)
    wq = b_wq.reshape(B_Q_RANK, B_HEADS, B_NOPE_DIM + B_ROPE_DIM)
    nope, r1, r2 = wq[..., :B_NOPE_DIM], wq[..., B_NOPE_DIM:B_NOPE_DIM + half], wq[..., B_NOPE_DIM + half:]
    zq64, zq32 = jnp.zeros_like(nope), jnp.zeros((B_Q_RANK, B_HEADS, 32), F32)
    wq1 = jnp.concatenate([nope, r1, r2, zq32], -1).reshape(B_Q_RANK, B_HEADS * B_PAD_DIM)
    wq2 = jnp.concatenate([zq64, r2, r1, zq32], -1).reshape(B_Q_RANK, B_HEADS * B_PAD_DIM)
    wkv = b_wkv.reshape(B_KV_RANK, B_HEADS, B_NOPE_DIM + B_V_DIM)
    wk = jnp.concatenate([wkv[..., :B_NOPE_DIM], jnp.zeros((B_KV_RANK, B_HEADS, 64), F32)], -1)
    wk = wk.reshape(B_KV_RANK, B_HEADS * B_PAD_DIM)
    wv = jnp.concatenate([wkv[..., B_NOPE_DIM:], jnp.zeros((B_KV_RANK, B_HEADS, 64), F32)], -1)
    wv = wv.reshape(B_KV_RANK, B_HEADS * LANE)
    blockdiag = lambda n: (jnp.arange(n)[:, None] // A_HEAD_DIM == jnp.arange(n)[None, :] // A_HEAD_DIM)
    return dict(
        win=win,
        wq=jnp.concatenate([wq1, wq2], 1).astype(BF16),
        wkv=jnp.concatenate([wk, wv], 1).astype(BF16),
        gqa=jnp.tile(a_qn, A_HEADS).reshape(1, -1), gka=jnp.tile(a_kn, A_KV_HEADS).reshape(1, -1),
        gqb=b_qn.reshape(1, -1), gkvb=b_kvn.reshape(1, -1),
        bdq=(blockdiag(A_HEADS * A_HEAD_DIM) / A_HEAD_DIM).astype(BF16),
        bdk=(blockdiag(A_KV_HEADS * A_HEAD_DIM) / A_HEAD_DIM).astype(BF16))


def kernel(x, c, ctx, c_ctx, ada_w, ada_b, norm_g, ff_w1, ff_w2, at_w_in, at_w_out, ga_q_norm, ga_k_norm,
           ml_q_norm, ml_kv_norm, ml_w_q_up, ml_w_kv_up, lc_w_in, lc_w_out, hy_short_w, hy_short_b, hy_w1,
           hy_b1, hy_w2, hy_b2, hy_w3, hy_freq, hy_skip, mb_conv_w, mb_conv_b, mb_a_log, mb_dt_bias, mb_d_skip,
           mb_norm):
    bsz, n_lat, d = x.shape
    n_ctx = ctx.shape[1]
    depth = ada_w.shape[0]
    rows = -(-(bsz + 1) // 8) * 8
    cond = jnp.concatenate([c, c_ctx[None], jnp.zeros((rows - bsz - 1, d), F32)], axis=0)
    mods = _ada_table(cond, ada_w, ada_b)
    rope_x = _rope_tables(n_lat, True)
    rope_c = _rope_tables(n_ctx, False)
    tm = 512
    for i in range(depth):
        need_ctx = i < depth - 1
        j = i // 2
        g = norm_g[i]
        mod_x = mods[i, :bsz].reshape(bsz, 6, d)
        mod_c = mods[i, bsz:bsz + 1].reshape(1, 6, d)
        if i % 2 == 0:
            ew = _even_weights(at_w_in[j], ga_q_norm[j], ga_k_norm[j], ml_q_norm[j], ml_kv_norm[j],
                               ml_w_q_up[j], ml_w_kv_up[j])
            px = _even_prep(x, mod_x, g, ew, rope_x, tm)
            pc = _even_prep(ctx, mod_c, g, ew, rope_c, tm)
            o1x, o2x = _attention(px[0:4:3], [px[1:3] + px[4:6], pc[1:3] + pc[4:6]], 512)
            if need_ctx:
                o1c, o2c = _attention(pc[0:4:3], [pc[1:3] + pc[4:6]], 256)
            w_out = at_w_out[j].astype(BF16)
        else:
            pad = jnp.zeros((d, LANE - ODD_DT), F32)
            win = jnp.concatenate([lc_w_in[j], pad], axis=1).astype(BF16)
            dtb = jnp.pad(mb_dt_bias[j].reshape(-1), (0, LANE - ODD_DT)).reshape(1, LANE)
            hy_x, z_x, xbc_x, dt_x = _odd_prep(x, mod_x, g, win, dtb, tm)
            hy_c, z_c, xbc_c, dt_c = _odd_prep(ctx, mod_c, g, win, dtb, tm)
            filt = (hy_w1[j], hy_b1[j], hy_w2[j], hy_b2[j], hy_w3[j], hy_freq[j])
            sw, sb = hy_short_w[j], hy_short_b[j].reshape(1, -1)
            o1x = _hyena(hy_x, sw, sb, hy_skip[j], filt)
            alog = jnp.pad(mb_a_log[j].reshape(-1), (0, LANE - ODD_DT)).reshape(1, LANE)
            dskip = jnp.repeat(mb_d_skip[j], D_HEAD_DIM).reshape(1, D_INNER)
            o2c, o2x = _ssd((xbc_c, dt_c, z_c), (xbc_x, dt_x, z_x), mb_conv_w[j], mb_conv_b[j].reshape(1, -1),
                            alog, dskip, mb_norm[j].reshape(1, -1), need_ctx)
            if need_ctx:
                o1c = _hyena(hy_c, sw, sb, hy_skip[j], filt)
            w_out = lc_w_out[j].astype(BF16)
        w1, w2 = ff_w1[i].astype(BF16), ff_w2[i].astype(BF16)
        x = _post(x, o1x, o2x, mod_x, g, w_out[:MIX_HALF], w_out[MIX_HALF:], w1, w2, tm, i > 0)
        if need_ctx:
            ctx = _post(ctx, o1c, o2c, mod_c, g, w_out[:MIX_HALF], w_out[MIX_HALF:], w1, w2, tm, i > 0)
    return x
```

```python
import functools
import math

import jax
import jax.numpy as jnp
from jax import lax
from jax.experimental import pallas as pl
from jax.experimental.pallas import tpu as pltpu

F32 = jnp.float32
BF16 = jnp.bfloat16

D_MODEL = 1024
GRID_W = 64
EPS = 1e-6
ROPE_THETA = 10000.0
MIX_HALF = D_MODEL // 2
A_HEAD_DIM = 64
A_HEADS = MIX_HALF // A_HEAD_DIM
A_KV_HEADS = A_HEADS // 4
A_GROUP = A_HEADS // A_KV_HEADS
B_NOPE_DIM = 64
B_ROPE_DIM = 32
B_V_DIM = 64
B_HEADS = MIX_HALF // B_V_DIM
B_Q_RANK = D_MODEL // 4
B_KV_RANK = D_MODEL // 8
B_PAD_DIM = 128
C_WIDTH = MIX_HALF
C_ORDER = 2
C_POS_EMB = 33
C_FILTER_HIDDEN = 64
C_DECAY_TARGET = 1e-2
C_DECAY_FRAC_SHORT = 0.3
C_DECAY_FRAC_LONG = 1.5
D_INNER = MIX_HALF
D_HEAD_DIM = 64
D_HEADS = D_INNER // D_HEAD_DIM
D_GROUPS = 2
D_STATE = 128
D_CHUNK = 128
FF_HIDDEN = 4 * D_MODEL
FF_CHUNK = 1024
A_SCALE = A_HEAD_DIM ** -0.5
B_SCALE = (B_NOPE_DIM + B_ROPE_DIM) ** -0.5
LOG2E = math.log2(math.e)
V_DIM = 64
ODD_HY = (C_ORDER + 1) * C_WIDTH
ODD_Z = D_INNER
ODD_XBC = D_INNER + 2 * D_GROUPS * D_STATE
ODD_DT = 2 * D_HEADS
LANE = 128
VMEM_LIMIT = 56 * 2 ** 20


def _params(*sem):
    return pltpu.CompilerParams(dimension_semantics=sem, vmem_limit_bytes=VMEM_LIMIT)


def _const_spec(shape):
    nd = len(shape)
    return pl.BlockSpec(shape, lambda *_: (0,) * nd, pipeline_mode=pl.Buffered(1))


def _dot(a, b):
    return jnp.dot(a, b, preferred_element_type=F32)


def _dot_nt(a, b):
    return lax.dot_general(a, b, (((1,), (1,)), ((), ())), preferred_element_type=F32)


def _dot_f32(a, b):
    return jnp.dot(a, b, preferred_element_type=F32, precision=lax.Precision.HIGHEST)


def _rms_rows(x):
    return x * lax.rsqrt(jnp.mean(x * x, axis=-1, keepdims=True) + EPS)


def _silu(x):
    return x * jax.nn.sigmoid(x)


def _swap_halves(x, half):
    n = x.shape[-1]
    lane = lax.broadcasted_iota(jnp.int32, x.shape, 1)
    r1 = pltpu.roll(x, half, 1)
    r2 = pltpu.roll(x, n - half, 1)
    i1 = pltpu.roll(lane, half, 1)
    return jnp.where(i1 == (lane ^ half), r1, r2)


def _row_neighbours(u):
    n = u.shape[0]
    row = lax.broadcasted_iota(jnp.int32, u.shape, 0)
    prev = jnp.where(row == 0, 0.0, pltpu.roll(u, 1, 0))
    nxt = jnp.where(row == n - 1, 0.0, pltpu.roll(u, n - 1, 0))
    return prev, nxt


def _short_conv(u, w_ref, b_ref):
    prev, nxt = _row_neighbours(u)
    return w_ref[0:1, :] * prev + w_ref[1:2, :] * u + w_ref[2:3, :] * nxt + b_ref[...]


def _ada_body(cond_ref, w_ref, b_ref, o_ref):
    h = _silu(cond_ref[...]).astype(BF16)
    o_ref[0] = _dot(h, w_ref[0].astype(BF16)) + b_ref[0]


def _ada_table(cond, ada_w, ada_b):
    depth, d, n = ada_w.shape
    r = cond.shape[0]
    tn = 1536
    return pl.pallas_call(
        _ada_body,
        grid=(depth, n // tn),
        in_specs=[pl.BlockSpec((r, d), lambda l, j: (0, 0)),
                  pl.BlockSpec((1, d, tn), lambda l, j: (l, 0, j)),
                  pl.BlockSpec((1, 1, tn), lambda l, j: (l, 0, j))],
        out_specs=pl.BlockSpec((1, r, tn), lambda l, j: (l, 0, j)),
        out_shape=jax.ShapeDtypeStruct((depth, r, n), F32),
        compiler_params=_params("arbitrary", "arbitrary"),
        name="ada_table",
    )(cond, ada_w, ada_b.reshape(depth, 1, n))


def _mod_index(per_batch):
    return (lambda b, i: (b, 0, 0)) if per_batch else (lambda b, i: (0, 0, 0))


def _norm_mod(x, g_row, mod, shift_row, scale_row):
    h = _rms_rows(x) * g_row
    return h * (1.0 + mod[scale_row:scale_row + 1]) + mod[shift_row:shift_row + 1]


def _even_prep_body(s_ref, mod_ref, g_ref, win_ref, wq_ref, wkv_ref, gqa_ref, gka_ref, gqb_ref, gkvb_ref,
                    bdq_ref, bdk_ref, ca_ref, sa_ref, cb_ref, sb_ref,
                    qa_ref, ka_ref, va_ref, qb_ref, kb_ref, vb_ref):
    h = _norm_mod(s_ref[0], g_ref[0:1], mod_ref[0], 0, 1).astype(BF16)
    p = _dot(h, win_ref[...])
    aq, bqa = p[:, 0:512], p[:, 512:768]
    ak, av, bkva = p[:, 768:896], p[:, 896:1152], p[:, 1152:1280]
    kr, krs = p[:, 1280:1408], p[:, 1408:1536]
    ca, sa, cb, sb = ca_ref[...], sa_ref[...], cb_ref[...], sb_ref[...]
    upper = lax.broadcasted_iota(jnp.int32, (h.shape[0], LANE), 1) >= V_DIM

    aqn = aq * lax.rsqrt(_dot((aq * aq).astype(BF16), bdq_ref[...]) + EPS) * gqa_ref[...]
    aqs = _swap_halves(aqn, A_HEAD_DIM // 2)
    for blk in range(4):
        sl = slice(blk * LANE, (blk + 1) * LANE)
        q2 = ((aqn[:, sl] * ca + aqs[:, sl] * sa) * (A_SCALE * LOG2E)).astype(BF16)
        qa_ref[0, 2 * blk] = q2[:, :A_HEAD_DIM]
        qa_ref[0, 2 * blk + 1] = q2[:, A_HEAD_DIM:]
    akn = ak * lax.rsqrt(_dot((ak * ak).astype(BF16), bdk_ref[...]) + EPS) * gka_ref[...]
    akr = (akn * ca + _swap_halves(akn, A_HEAD_DIM // 2) * sa).astype(BF16)
    for j in range(A_KV_HEADS):
        ka_ref[0, j] = akr[:, j * A_HEAD_DIM:(j + 1) * A_HEAD_DIM]
        va_ref[0, j] = jnp.where(upper, 1.0, av[:, j * LANE:(j + 1) * LANE]).astype(BF16)

    bqn = (_rms_rows(bqa) * gqb_ref[...]).astype(BF16)
    nq = B_HEADS * B_PAD_DIM
    u = _dot(bqn, wq_ref[:, :nq])
    us = _dot(bqn, wq_ref[:, nq:])
    bkvn = (_rms_rows(bkva) * gkvb_ref[...]).astype(BF16)
    uk = _dot(bkvn, wkv_ref[:, :nq])
    uv = _dot(bkvn, wkv_ref[:, nq:])
    krr = kr * cb + krs * sb
    for hd in range(B_HEADS):
        sl = slice(hd * B_PAD_DIM, (hd + 1) * B_PAD_DIM)
        qb_ref[0, hd] = ((u[:, sl] * cb + us[:, sl] * sb) * (B_SCALE * LOG2E)).astype(BF16)
        kb_ref[0, hd] = (uk[:, sl] + krr).astype(BF16)
        vb_ref[0, hd] = jnp.where(upper, 1.0, uv[:, sl]).astype(BF16)


def _even_prep(s, mod, g, ew, rope, tm):
    bsz, t, d = s.shape
    tm = min(tm, t)
    per_batch = mod.shape[0] != 1
    row = lambda b, i: (b, i, 0)
    hrow = lambda b, i: (b, 0, i, 0)
    tab = lambda b, i: (i, 0)
    consts = [ew["win"], ew["wq"], ew["wkv"], ew["gqa"], ew["gka"], ew["gqb"], ew["gkvb"], ew["bdq"], ew["bdk"]]
    outs = [(A_HEADS, A_HEAD_DIM), (A_KV_HEADS, A_HEAD_DIM), (A_KV_HEADS, LANE),
            (B_HEADS, B_PAD_DIM), (B_HEADS, B_PAD_DIM), (B_HEADS, LANE)]
    return pl.pallas_call(
        _even_prep_body,
        grid=(bsz, t // tm),
        in_specs=[pl.BlockSpec((1, tm, d), row),
                  pl.BlockSpec((1, 6, d), _mod_index(per_batch)),
                  _const_spec(g.shape)]
        + [_const_spec(c.shape) for c in consts]
        + [pl.BlockSpec((tm, LANE), tab)] * 4,
        out_specs=[pl.BlockSpec((1, nh, tm, hd), hrow) for nh, hd in outs],
        out_shape=[jax.ShapeDtypeStruct((bsz, nh, t, hd), BF16) for nh, hd in outs],
        compiler_params=_params("parallel", "arbitrary"),
        name="even_prep",
    )(s, mod, g, *consts, *rope)


def _softmax_attend(q, kvs):
    ss = [_dot_nt(q, k) for k, _ in kvs]
    m = ss[0].max(axis=-1, keepdims=True)
    for s in ss[1:]:
        m = jnp.maximum(m, s.max(axis=-1, keepdims=True))
    acc = None
    for s, (_, v) in zip(ss, kvs):
        o = _dot(jnp.exp2(s - m).astype(BF16), v)
        acc = o if acc is None else acc + o
    return acc / pltpu.roll(acc, V_DIM, 1)


def _attn_body(n_src, qa_ref, qb_ref, *refs):
    kv_refs = refs[:4 * n_src]
    oa_ref, ob_ref = refs[4 * n_src:]
    srcs = [kv_refs[4 * i:4 * i + 4] for i in range(n_src)]
    lower = lax.broadcasted_iota(jnp.int32, (qa_ref.shape[2], LANE), 1) < V_DIM

    def pair_out(first, second):
        return jnp.where(lower, first, pltpu.roll(second, V_DIM, 1)).astype(BF16)

    for pair in range(A_HEADS // 2):
        outs = []
        for hd in (2 * pair, 2 * pair + 1):
            j = hd // A_GROUP
            outs.append(_softmax_attend(qa_ref[0, hd], [(ka[0, j], va[0, j]) for ka, va, _, _ in srcs]))
        oa_ref[0, :, pair * LANE:(pair + 1) * LANE] = pair_out(*outs)
    for pair in range(B_HEADS // 2):
        outs = []
        for hd in (2 * pair, 2 * pair + 1):
            outs.append(_softmax_attend(qb_ref[0, hd], [(kb[0, hd], vb[0, hd]) for _, _, kb, vb in srcs]))
        ob_ref[0, :, pair * LANE:(pair + 1) * LANE] = pair_out(*outs)


def _attention(q, kv_srcs, tq):
    qa, qb = q
    bsz, _, t, _ = qa.shape
    tq = min(tq, t)
    qrow = lambda b, i: (b, 0, i, 0)
    whole = lambda b, i: (b, 0, 0, 0)
    in_specs = [pl.BlockSpec((1, A_HEADS, tq, A_HEAD_DIM), qrow), pl.BlockSpec((1, B_HEADS, tq, B_PAD_DIM), qrow)]
    args = [qa, qb]
    for src in kv_srcs:
        for a in src:
            in_specs.append(pl.BlockSpec((1,) + a.shape[1:], whole))
            args.append(a)
    out = jax.ShapeDtypeStruct((bsz, t, MIX_HALF), BF16)
    return pl.pallas_call(
        functools.partial(_attn_body, len(kv_srcs)),
        grid=(bsz, t // tq),
        in_specs=in_specs,
        out_specs=[pl.BlockSpec((1, tq, MIX_HALF), lambda b, i: (b, i, 0))] * 2,
        out_shape=[out, out],
        compiler_params=_params("parallel", "arbitrary"),
        name="attention",
    )(*args)


def _post_body(s_ref, o1_ref, o2_ref, mod_ref, g_ref, wo1_ref, wo2_ref, w1_ref, w2_ref, out_ref):
    mod = mod_ref[0]
    mo = _dot(o1_ref[0], wo1_ref[...]) + _dot(o2_ref[0], wo2_ref[...])
    x1 = s_ref[0] + mod[2:3] * (_rms_rows(mo) * g_ref[1:2])
    h = _norm_mod(x1, g_ref[2:3], mod, 3, 4).astype(BF16)
    acc = None
    for c in range(FF_HIDDEN // FF_CHUNK):
        sl = slice(c * FF_CHUNK, (c + 1) * FF_CHUNK)
        u = jnp.square(jnp.maximum(_dot(h, w1_ref[:, sl]), 0.0)).astype(BF16)
        part = _dot(u, w2_ref[sl, :])
        acc = part if acc is None else acc + part
    out_ref[0] = x1 + mod[5:6] * (_rms_rows(acc) * g_ref[3:4])


def _post(s, o1, o2, mod, g, wo1, wo2, w1, w2, tm, in_place):
    bsz, t, d = s.shape
    tm = min(tm, t)
    per_batch = mod.shape[0] != 1
    row = lambda b, i: (b, i, 0)
    return pl.pallas_call(
        _post_body,
        grid=(bsz, t // tm),
        in_specs=[pl.BlockSpec((1, tm, d), row),
                  pl.BlockSpec((1, tm, MIX_HALF), row),
                  pl.BlockSpec((1, tm, MIX_HALF), row),
                  pl.BlockSpec((1, 6, d), _mod_index(per_batch)),
                  _const_spec(g.shape), _const_spec(wo1.shape), _const_spec(wo2.shape),
                  _const_spec(w1.shape), _const_spec(w2.shape)],
        out_specs=pl.BlockSpec((1, tm, d), row),
        out_shape=jax.ShapeDtypeStruct(s.shape, F32),
        input_output_aliases={0: 0} if in_place else {},
        compiler_params=_params("parallel", "arbitrary"),
        name="post",
    )(s, o1, o2, mod, g, wo1, wo2, w1, w2)


ODD_NIN = ODD_HY + ODD_Z + ODD_XBC + LANE


def _softplus(x):
    return jnp.maximum(x, 0.0) + jnp.log1p(jnp.exp(-jnp.abs(x)))


def _odd_prep_body(s_ref, mod_ref, g_ref, win_ref, dtb_ref, hy_ref, z_ref, xbc_ref, dt_ref):
    h = _norm_mod(s_ref[0], g_ref[0:1], mod_ref[0], 0, 1).astype(BF16)
    p = _dot(h, win_ref[...])
    hy_ref[0] = p[:, :ODD_HY].astype(BF16)
    z_ref[0] = p[:, ODD_HY:ODD_HY + ODD_Z].astype(BF16)
    xbc_ref[0] = p[:, ODD_HY + ODD_Z:ODD_HY + ODD_Z + ODD_XBC].astype(BF16)
    dt_ref[0] = _softplus(p[:, ODD_HY + ODD_Z + ODD_XBC:] + dtb_ref[...])


def _odd_prep(s, mod, g, win, dtb, tm):
    bsz, t, d = s.shape
    tm = min(tm, t)
    per_batch = mod.shape[0] != 1
    row = lambda b, i: (b, i, 0)
    widths = [(ODD_HY, BF16), (ODD_Z, BF16), (ODD_XBC, BF16), (LANE, F32)]
    return pl.pallas_call(
        _odd_prep_body,
        grid=(bsz, t // tm),
        in_specs=[pl.BlockSpec((1, tm, d), row),
                  pl.BlockSpec((1, 6, d), _mod_index(per_batch)),
                  _const_spec(g.shape), _const_spec(win.shape), _const_spec(dtb.shape)],
        out_specs=[pl.BlockSpec((1, tm, w), row) for w, _ in widths],
        out_shape=[jax.ShapeDtypeStruct((bsz, t, w), dt) for w, dt in widths],
        compiler_params=_params("parallel", "arbitrary"),
        name="odd_prep",
    )(s, mod, g, win, dtb)


def _filter_body(n, z_ref, w1_ref, b1_ref, w2_ref, b2_ref, w3_ref, fr_ref, dec_ref, ts_ref, rs_ref):
    fr = fr_ref[...]
    h = jnp.sin(fr * (_dot_f32(z_ref[...], w1_ref[...]) + b1_ref[...]))
    h = jnp.sin(fr * (_dot_f32(h, w2_ref[...]) + b2_ref[...]))
    taps = _dot_f32(h, w3_ref[0, 0])
    tn = dec_ref.shape[1]
    fwd, bwd = taps[:, :tn] * dec_ref[...], taps[:, tn:] * dec_ref[...]
    lag = lax.broadcasted_iota(jnp.int32, fwd.shape, 0) + (pl.program_id(0) * fwd.shape[0] - n)
    ts_ref[0] = jnp.where(lag == -n, 0.0, jnp.where(lag >= 0, fwd, bwd))
    rs_ref[0] = jnp.where(lag == -n, 0.0, jnp.where(lag > 0, bwd, fwd))


def _hyena_taps(n, w1, b1, w2, b2, w3, freq):
    pos = jnp.minimum(jnp.abs(jnp.arange(2 * n) - n), n - 1)
    t = jnp.linspace(0.0, 1.0, n, dtype=F32)[pos]
    bands = (C_POS_EMB - 1) // 2
    w = (2.0 * math.pi * jnp.arange(n, dtype=F32) / n)[pos]
    fb = jnp.linspace(1e-4, bands - 1, bands, dtype=F32)
    ph = w[:, None] * fb[None, :]
    z = jnp.concatenate([t[:, None], jnp.cos(ph), -jnp.sin(ph)], axis=-1)
    decay_max = math.log(C_DECAY_TARGET) / C_DECAY_FRAC_SHORT
    decay_min = math.log(C_DECAY_TARGET) / C_DECAY_FRAC_LONG
    deltas = jnp.abs(jnp.linspace(decay_min, decay_max, C_WIDTH, dtype=F32))
    dec = jnp.exp(-t[:, None] * deltas[None, :])
    pad_h = LANE - C_FILTER_HIDDEN
    zp = jnp.pad(z, ((0, 0), (0, LANE - C_POS_EMB)))
    w1p = jnp.pad(w1, ((0, LANE - C_POS_EMB), (0, pad_h)))
    w2p = jnp.pad(w2, ((0, pad_h), (0, pad_h)))
    tn = 256
    nbc = C_WIDTH // tn
    w3p = jnp.pad(w3, ((0, pad_h), (0, 0))).reshape(LANE, C_ORDER, 2, nbc, tn)
    w3p = w3p.transpose(1, 3, 0, 2, 4).reshape(C_ORDER, nbc, LANE, 2 * tn)
    row = lambda v: jnp.pad(v, (0, pad_h)).reshape(1, LANE)
    tr = min(2 * n, 512)
    const = lambda shape: pl.BlockSpec(shape, lambda i, o, j: (0,) * len(shape))
    out = jax.ShapeDtypeStruct((C_ORDER, 2 * n, C_WIDTH), F32)
    return pl.pallas_call(
        functools.partial(_filter_body, n),
        grid=(2 * n // tr, C_ORDER, nbc),
        in_specs=[pl.BlockSpec((tr, LANE), lambda i, o, j: (i, 0)),
                  const((LANE, LANE)), const((1, LANE)), const((LANE, LANE)), const((1, LANE)),
                  pl.BlockSpec((1, 1, LANE, 2 * tn), lambda i, o, j: (o, j, 0, 0)),
                  const((1, LANE)),
                  pl.BlockSpec((tr, tn), lambda i, o, j: (i, j))],
        out_specs=[pl.BlockSpec((1, tr, tn), lambda i, o, j: (o, i, j))] * 2,
        out_shape=[out, out],
        compiler_params=_params("arbitrary", "arbitrary", "arbitrary"),
        name="hyena_taps",
    )(zp, w1p, row(b1), w2p, row(b2), w3p, row(freq), dec)


def _dft_matrices(n):
    m = 2 * n
    k = jnp.arange(n, dtype=jnp.int32)[:, None]
    s = jnp.arange(n, dtype=jnp.int32)[None, :]
    ang = ((k * s) % m).astype(F32) * (2.0 * math.pi / m)
    nyq = jnp.where(s % 2 == 0, 1.0, -1.0).astype(F32)
    cosm = jnp.cos(ang)
    sinm = jnp.where(k == 0, nyq, -jnp.sin(ang))
    fwd = jnp.concatenate([cosm, sinm], axis=0)
    wgt = jnp.where(k == 0, 1.0 / m, 2.0 / m)
    inv = jnp.concatenate([(cosm * wgt).T, (sinm * wgt).T], axis=1)
    return fwd.astype(BF16), inv.astype(BF16)


DFT_ROWS = 256


def _first_row(shape, offset=0):
    return lax.broadcasted_iota(jnp.int32, shape, 0) + offset == 0


def _dft_row_tiles(n, step):
    tr = min(DFT_ROWS, n)

    def body(r, carry):
        r0 = pl.multiple_of(r * tr, tr)
        step(r0, pl.ds(r0, tr), pl.ds(pl.multiple_of(n + r0, tr), tr))
        return carry
    lax.fori_loop(0, n // tr, body, 0)


def _spectrum_body(n, f_ref, fw_ref, bw_ref, o_ref, fs_ref, bs_ref):
    fw = fw_ref[0]
    bw = jnp.where(_first_row(fw.shape), 0.0, bw_ref[0])
    for u, s_ref in ((fw, fs_ref), (bw, bs_ref)):
        hi = u.astype(BF16)
        s_ref[0] = hi
        s_ref[1] = (u - hi.astype(F32)).astype(BF16)

    def step(r0, rows, rows_im):
        def packed(s_ref, rr):
            return _dot(f_ref[rr, :], s_ref[0]) + _dot(f_ref[rr, :], s_ref[1])
        o_ref[0, 0, rows, :] = packed(fs_ref, rows) + packed(bs_ref, rows)
        im1, im2 = packed(fs_ref, rows_im), packed(bs_ref, rows_im)
        o_ref[0, 0, rows_im, :] = jnp.where(_first_row(im1.shape, r0), im1 + im2, im1 - im2)
    _dft_row_tiles(n, step)


def _hyena_spectrum(ts, rs, fwd, tb, cw):
    n = ts.shape[1] // 2
    nb = n // tb
    nd = 2 * nb - 1
    return pl.pallas_call(
        functools.partial(_spectrum_body, tb),
        grid=(C_ORDER, nd, C_WIDTH // cw),
        in_specs=[_const_spec(fwd.shape),
                  pl.BlockSpec((1, tb, cw), lambda o, di, j: (o, di + 1, j)),
                  pl.BlockSpec((1, tb, cw), lambda o, di, j: (o, 2 * nb - 1 - di, j))],
        out_specs=pl.BlockSpec((1, 1, 2 * tb, cw), lambda o, di, j: (o, di, 0, j)),
        out_shape=jax.ShapeDtypeStruct((C_ORDER, nd, 2 * tb, C_WIDTH), F32),
        scratch_shapes=[pltpu.VMEM((2, tb, cw), BF16), pltpu.VMEM((2, tb, cw), BF16)],
        compiler_params=_params("arbitrary", "arbitrary", "arbitrary"),
        name="hyena_spectrum",
    )(fwd, ts, rs)


HY_BLOCK = 512
HY_TILE = 64


def _hy_conv_body(n, tb, prev_conv, prev_ref, pw_ref, pb_ref, gate_ref, gw_ref, gb_ref, skip_ref, f_ref, g_ref,
                  h_ref, o_ref, zb_ref, u_ref, yb_ref, gate_s, skip_s):
    nb = n // tb
    prev = prev_ref[0].astype(F32)
    if prev_conv:
        prev = _short_conv(prev, pw_ref, pb_ref)
    zb_ref[...] = prev.astype(BF16)
    gate = _short_conv(gate_ref[0].astype(F32), gw_ref, gb_ref)
    gate_s[...] = gate
    skip_s[...] = gate * (prev * skip_ref[0])
    for j in range(nb):
        u_ref[j] = _dot(f_ref[...], zb_ref[j * tb:(j + 1) * tb, :])

    def mac_tile(i, r0):
        rows, rows_im = pl.ds(r0, HY_TILE), pl.ds(tb + r0, HY_TILE)
        re_acc = im_acc = None
        for j in range(nb):
            di = i - j + nb - 1
            ur, ui = u_ref[j, rows, :], u_ref[j, rows_im, :]
            hr, hi = h_ref[0, di, rows, :], h_ref[0, di, rows_im, :]
            imim = ui * hi
            if r0 == 0:
                first = _first_row(ur.shape)
                re_t = ur * hr - jnp.where(first, 0.0, imim)
                im_t = jnp.where(first, imim, ur * hi + ui * hr)
            else:
                re_t = ur * hr - imim
                im_t = ur * hi + ui * hr
            re_acc = re_t if re_acc is None else re_acc + re_t
            im_acc = im_t if im_acc is None else im_acc + im_t
        yb_ref[i, rows, :] = re_acc.astype(BF16)
        yb_ref[i, rows_im, :] = im_acc.astype(BF16)

    for i in range(nb):
        for r in range(tb // HY_TILE):
            mac_tile(i, r * HY_TILE)
        y = _dot(g_ref[...], yb_ref[i])
        sl = slice(i * tb, (i + 1) * tb)
        o_ref[0, sl, :] = (gate_s[sl, :] * y + skip_s[sl, :]).astype(BF16)


def _hy_conv(prev, prev_part, prev_conv, hy, gate_part, conv_w, conv_b, skip, fwd, inv, spec, order, tb, cw):
    bsz, n, _ = hy.shape
    nb = C_WIDTH // cw
    nd = spec.shape[1]
    return pl.pallas_call(
        functools.partial(_hy_conv_body, n, tb, prev_conv),
        grid=(nb, bsz),
        in_specs=[pl.BlockSpec((1, n, cw), lambda j, b: (b, 0, prev_part * nb + j)),
                  pl.BlockSpec((3, cw), lambda j, b: (0, prev_part * nb + j)),
                  pl.BlockSpec((1, cw), lambda j, b: (0, prev_part * nb + j)),
                  pl.BlockSpec((1, n, cw), lambda j, b: (b, 0, gate_part * nb + j)),
                  pl.BlockSpec((3, cw), lambda j, b: (0, gate_part * nb + j)),
                  pl.BlockSpec((1, cw), lambda j, b: (0, gate_part * nb + j)),
                  pl.BlockSpec((1, 1, cw), lambda j, b: (order, 0, j)),
                  _const_spec(fwd.shape), _const_spec(inv.shape),
                  pl.BlockSpec((1, nd, 2 * tb, cw), lambda j, b: (order, 0, 0, j))],
        out_specs=pl.BlockSpec((1, n, cw), lambda j, b: (b, 0, j)),
        out_shape=jax.ShapeDtypeStruct((bsz, n, C_WIDTH), BF16),
        scratch_shapes=[pltpu.VMEM((n, cw), BF16), pltpu.VMEM((n // tb, 2 * tb, cw), F32),
                        pltpu.VMEM((n // tb, 2 * tb, cw), BF16), pltpu.VMEM((n, cw), F32), pltpu.VMEM((n, cw), F32)],
        compiler_params=_params("parallel", "arbitrary"),
        name="hyena_conv",
    )(prev, conv_w, conv_b, hy, conv_w, conv_b, skip, fwd, inv, spec)


def _hyena(hy, conv_w, conv_b, skip, filt):
    n = hy.shape[1]
    tb = min(HY_BLOCK, n)
    cw = 256
    fwd, inv = _dft_matrices(tb)
    spec = _hyena_spectrum(*_hyena_taps(n, *filt), fwd, tb, cw)
    skip3 = skip.reshape(C_ORDER, 1, C_WIDTH)
    z1 = _hy_conv(hy, 0, True, hy, 1, conv_w, conv_b, skip3, fwd, inv, spec, 0, tb, cw)
    return _hy_conv(z1, 0, False, hy, 2, conv_w, conv_b, skip3, fwd, inv, spec, 1, tb, cw)


def _split2(x):
    hi = x.astype(BF16)
    return hi, (x - hi.astype(F32)).astype(BF16)


def _select_cols(x, sel2):
    return _dot(jnp.concatenate(_split2(x), axis=1), sel2)


def _chunk_cumsum(adt, tri):
    hi = adt.astype(BF16)
    r1 = adt - hi.astype(F32)
    mid = r1.astype(BF16)
    lo = (r1 - mid.astype(F32)).astype(BF16)
    s = _dot(tri, jnp.concatenate([hi, mid, lo], axis=1))
    return s[:, :LANE] + s[:, LANE:2 * LANE] + s[:, 2 * LANE:]


def _ssd_chunk(act_ref, dt_ref, y_ref, st_ref, ehead_ref, erep_ref, start, d, a_row, want_y):
    n = D_CHUNK
    rows = pl.ds(start, n)
    dtc = dt_ref[0, rows, :]
    r = lax.broadcasted_iota(jnp.int32, (n, n), 0)
    c = lax.broadcasted_iota(jnp.int32, (n, n), 1)
    causal = (r >= c) if d == 0 else (r <= c)
    acs = _chunk_cumsum(dtc * a_row, causal.astype(BF16))
    last = n - 1 if d == 0 else 0
    to_end = jnp.exp(acs[last:last + 1, :] - acs)
    spread = _select_cols(jnp.concatenate([dtc, jnp.exp(acs), to_end], axis=0), ehead_ref[d])
    dt_x, ea_x, te_x = spread[0:n], spread[n:2 * n], spread[2 * n:]
    xdt = act_ref[rows, 0:D_INNER] * dt_x
    xw = (xdt * te_x).astype(BF16)
    xdt = xdt.astype(BF16)
    gw = D_INNER // D_GROUPS
    heads_per_group = D_HEADS // D_GROUPS
    if want_y:
        acs_t = acs.T
        acs2 = jnp.concatenate(_split2(acs), axis=1)
        lower = lax.broadcasted_iota(jnp.int32, (n, LANE), 1) < D_HEAD_DIM
    y_blocks, carried = [], []
    for g in range(D_GROUPS):
        b0 = D_INNER + g * D_STATE
        c0 = D_INNER + D_GROUPS * D_STATE + g * D_STATE
        bg = act_ref[rows, b0:b0 + D_STATE].astype(BF16)
        st = st_ref[g]
        if want_y:
            cg = act_ref[rows, c0:c0 + D_STATE].astype(BF16)
            gram = _dot_nt(cg, bg)
            carried.append(_dot(cg, st.astype(BF16)))
            for pr in range(heads_per_group // 2):
                pair = g * (heads_per_group // 2) + pr
                pcol = d * (D_HEADS // 2) + pair
                rep = _dot(acs2, erep_ref[pcol])
                weights = []
                for e in range(2):
                    col = 2 * pcol + e
                    decay = jnp.where(causal, jnp.exp(rep[:, e * LANE:(e + 1) * LANE] - acs_t[col:col + 1, :]), 0.0)
                    weights.append((gram * decay).astype(BF16))
                yy = _dot(jnp.concatenate(weights, axis=0), xdt[:, pair * LANE:(pair + 1) * LANE])
                y_blocks.append(jnp.where(lower, yy[0:n], yy[n:]))
        upd = lax.dot_general(bg, xw[:, g * gw:(g + 1) * gw], (((0,), (0,)), ((), ())),
                              preferred_element_type=F32)
        st_ref[g] = ea_x[last:last + 1, g * gw:(g + 1) * gw] * st + upd
    if want_y:
        y = jnp.concatenate(y_blocks, axis=-1) + ea_x * jnp.concatenate(carried, axis=-1)
        y_ref[rows, :] = y_ref[rows, :] + y


def _ssd_body(need_ctx, *refs):
    if need_ctx:
        (xc_ref, dtc_ref, zc_ref, xx_ref, dtx_ref, zx_ref, cw_ref, cb_ref, alog_ref, dskip_ref, mn_ref,
         ehead_ref, erep_ref, oc_ref, ox_ref, actc_ref, actx_ref, yc_ref, yx_ref, st_ref) = refs
    else:
        (xc_ref, dtc_ref, xx_ref, dtx_ref, zx_ref, cw_ref, cb_ref, alog_ref, dskip_ref, mn_ref,
         ehead_ref, erep_ref, ox_ref, actc_ref, actx_ref, yx_ref, st_ref) = refs
        zc_ref = oc_ref = yc_ref = None
    for src, act in ((xc_ref, actc_ref), (xx_ref, actx_ref)):
        for blk in range(ODD_XBC // LANE):
            sl = slice(blk * LANE, (blk + 1) * LANE)
            v = _short_conv(src[0, :, sl].astype(F32), cw_ref.at[:, sl], cb_ref.at[:, sl])
            act[:, sl] = _silu(v)
    a_row = -jnp.exp(alog_ref[...])
    nc_c = actc_ref.shape[0] // D_CHUNK
    nc_x = actx_ref.shape[0] // D_CHUNK
    st_ref[...] = jnp.zeros_like(st_ref)
    yx_ref[...] = jnp.zeros_like(yx_ref)
    if need_ctx:
        yc_ref[...] = jnp.zeros_like(yc_ref)

    def ctx_step(k, carry):
        for d, kk in ((0, k), (1, nc_c - 1 - k)):
            _ssd_chunk(actc_ref, dtc_ref, yc_ref, st_ref.at[d], ehead_ref, erep_ref,
                       pl.multiple_of(kk * D_CHUNK, D_CHUNK), d, a_row, need_ctx)
        return carry

    def lat_step(k, carry):
        for d, kk in ((0, k), (1, nc_x - 1 - k)):
            _ssd_chunk(actx_ref, dtx_ref, yx_ref, st_ref.at[d], ehead_ref, erep_ref,
                       pl.multiple_of(kk * D_CHUNK, D_CHUNK), d, a_row, True)
        return carry

    lax.fori_loop(0, nc_c, ctx_step, 0)
    lax.fori_loop(0, nc_x, lat_step, 0, unroll=4)

    def finish(act, y_ref, z_ref, o_ref):
        def step(k, carry):
            rows = pl.ds(pl.multiple_of(k * D_CHUNK, D_CHUNK), D_CHUNK)
            y = y_ref[rows, :] + act[rows, 0:D_INNER] * dskip_ref[...]
            yz = y * _silu(z_ref[0, rows, :].astype(F32))
            o_ref[0, rows, :] = (_rms_rows(yz) * mn_ref[...]).astype(BF16)
            return carry
        lax.fori_loop(0, act.shape[0] // D_CHUNK, step, 0)

    finish(actx_ref, yx_ref, zx_ref, ox_ref)
    if need_ctx:
        finish(actc_ref, yc_ref, zc_ref, oc_ref)


def _ssd_selectors():
    lane = jnp.arange(LANE)[:, None]
    head = jnp.arange(D_INNER)[None, :] // D_HEAD_DIM
    ehead = jnp.stack([lane == d * D_HEADS + head for d in range(2)])
    pair_col = 2 * jnp.arange(ODD_DT // 2)[:, None, None] + jnp.arange(2 * LANE)[None, None, :] // LANE
    erep = lane[None] == pair_col
    twice = lambda sel: jnp.concatenate([sel, sel], axis=1).astype(BF16)
    return twice(ehead), twice(erep)


def _ssd(ctx_in, lat_in, conv_w, conv_b, alog, dskip, mnorm, need_ctx):
    xc, dtc, zc = ctx_in
    xx, dtx, zx = lat_in
    bsz, n, _ = xx.shape
    nc = xc.shape[1]
    blk = lambda a: pl.BlockSpec((1,) + a.shape[1:], lambda b: (b, 0, 0))
    seqs = [xc, dtc] + ([zc] if need_ctx else []) + [xx, dtx, zx]
    consts = [conv_w, conv_b, alog, dskip, mnorm, *_ssd_selectors()]
    out_x = jax.ShapeDtypeStruct((bsz, n, D_INNER), BF16)
    out_c = jax.ShapeDtypeStruct((bsz, nc, D_INNER), BF16)
    out_shape = [out_c, out_x] if need_ctx else [out_x]
    scratch = [pltpu.VMEM((nc, ODD_XBC), F32), pltpu.VMEM((n, ODD_XBC), F32)]
    scratch += ([pltpu.VMEM((nc, D_INNER), F32)] if need_ctx else []) + [pltpu.VMEM((n, D_INNER), F32)]
    scratch += [pltpu.VMEM((2, D_GROUPS, D_STATE, D_INNER // D_GROUPS), F32)]
    outs = pl.pallas_call(
        functools.partial(_ssd_body, need_ctx),
        grid=(bsz,),
        in_specs=[blk(a) for a in seqs] + [_const_spec(c.shape) for c in consts],
        out_specs=[blk(o) for o in out_shape],
        out_shape=out_shape,
        scratch_shapes=scratch,
        compiler_params=_params("parallel"),
        name="ssd",
    )(*seqs, *consts)
    return (outs[0], outs[1]) if need_ctx else (None, outs[0])


def _axial_rope(n_tokens, rot_dim):
    rows = n_tokens // GRID_W
    r_idx, c_idx = jnp.meshgrid(jnp.arange(rows), jnp.arange(GRID_W), indexing="ij")
    quarter = rot_dim // 4
    inv_freq = ROPE_THETA ** (-jnp.arange(quarter, dtype=F32) / quarter)
    ang = jnp.concatenate([r_idx.reshape(-1, 1).astype(F32) * inv_freq,
                           c_idx.reshape(-1, 1).astype(F32) * inv_freq], axis=-1)
    return jnp.cos(ang), jnp.sin(ang)


def _rope_tables(n_tokens, rotate):
    ones, zeros = jnp.ones((n_tokens, 64), F32), jnp.zeros((n_tokens, 64), F32)
    if rotate:
        ca, sa = _axial_rope(n_tokens, A_HEAD_DIM)
        cb, sb = _axial_rope(n_tokens, B_ROPE_DIM)
    else:
        ca, sa = jnp.ones((n_tokens, 32), F32), jnp.zeros((n_tokens, 32), F32)
        cb, sb = jnp.ones((n_tokens, 16), F32), jnp.zeros((n_tokens, 16), F32)
    z32 = zeros[:, :32]
    return (jnp.tile(jnp.concatenate([ca, ca], -1), (1, 2)),
            jnp.tile(jnp.concatenate([-sa, sa], -1), (1, 2)),
            jnp.concatenate([ones, cb, cb, z32], -1),
            jnp.concatenate([zeros, -sb, sb, z32], -1))


def _even_weights(w_in, a_qn, a_kn, b_qn, b_kvn, b_wq, b_wkv):
    d = w_in.shape[0]
    ak, av, bkva, bkr, aq, bqa = jnp.split(w_in, [128, 256, 384, 416, 928], axis=1)
    z64, z32 = jnp.zeros((d, 64), F32), jnp.zeros((d, 32), F32)
    half = B_ROPE_DIM // 2
    kr = jnp.concatenate([z64, bkr, z32], 1)
    krs = jnp.concatenate([z64, bkr[:, half:], bkr[:, :half], z32], 1)
    av0 = jnp.concatenate([av[:, :A_HEAD_DIM], z64], 1)
    av1 = jnp.concatenate([av[:, A_HEAD_DIM:], z64], 1)
    win = jnp.concatenate([aq, bqa, ak, av0, av1, bkva, kr, krs], 1).astype(BF16)
    wq = b_wq.reshape(B_Q_RANK, B_HEADS, B_NOPE_DIM + B_ROPE_DIM)
    nope, r1, r2 = wq[..., :B_NOPE_DIM], wq[..., B_NOPE_DIM:B_NOPE_DIM + half], wq[..., B_NOPE_DIM + half:]
    zq64, zq32 = jnp.zeros_like(nope), jnp.zeros((B_Q_RANK, B_HEADS, 32), F32)
    wq1 = jnp.concatenate([nope, r1, r2, zq32], -1).reshape(B_Q_RANK, B_HEADS * B_PAD_DIM)
    wq2 = jnp.concatenate([zq64, r2, r1, zq32], -1).reshape(B_Q_RANK, B_HEADS * B_PAD_DIM)
    wkv = b_wkv.reshape(B_KV_RANK, B_HEADS, B_NOPE_DIM + B_V_DIM)
    wk = jnp.concatenate([wkv[..., :B_NOPE_DIM], jnp.zeros((B_KV_RANK, B_HEADS, 64), F32)], -1)
    wk = wk.reshape(B_KV_RANK, B_HEADS * B_PAD_DIM)
    wv = jnp.concatenate([wkv[..., B_NOPE_DIM:], jnp.zeros((B_KV_RANK, B_HEADS, 64), F32)], -1)
    wv = wv.reshape(B_KV_RANK, B_HEADS * LANE)
    blockdiag = lambda n: (jnp.arange(n)[:, None] // A_HEAD_DIM == jnp.arange(n)[None, :] // A_HEAD_DIM)
    return dict(
        win=win,
        wq=jnp.concatenate([wq1, wq2], 1).astype(BF16),
        wkv=jnp.concatenate([wk, wv], 1).astype(BF16),
        gqa=jnp.tile(a_qn, A_HEADS).reshape(1, -1), gka=jnp.tile(a_kn, A_KV_HEADS).reshape(1, -1),
        gqb=b_qn.reshape(1, -1), gkvb=b_kvn.reshape(1, -1),
        bdq=(blockdiag(A_HEADS * A_HEAD_DIM) / A_HEAD_DIM).astype(BF16),
        bdk=(blockdiag(A_KV_HEADS * A_HEAD_DIM) / A_HEAD_DIM).astype(BF16))


def kernel(x, c, ctx, c_ctx, ada_w, ada_b, norm_g, ff_w1, ff_w2, at_w_in, at_w_out, ga_q_norm, ga_k_norm,
           ml_q_norm, ml_kv_norm, ml_w_q_up, ml_w_kv_up, lc_w_in, lc_w_out, hy_short_w, hy_short_b, hy_w1,
           hy_b1, hy_w2, hy_b2, hy_w3, hy_freq, hy_skip, mb_conv_w, mb_conv_b, mb_a_log, mb_dt_bias, mb_d_skip,
           mb_norm):
    bsz, n_lat, d = x.shape
    n_ctx = ctx.shape[1]
    depth = ada_w.shape[0]
    rows = -(-(bsz + 1) // 8) * 8
    cond = jnp.concatenate([c, c_ctx[None], jnp.zeros((rows - bsz - 1, d), F32)], axis=0)
    mods = _ada_table(cond, ada_w, ada_b)
    rope_x = _rope_tables(n_lat, True)
    rope_c = _rope_tables(n_ctx, False)
    tm = 512
    for i in range(depth):
        need_ctx = i < depth - 1
        j = i // 2
        g = norm_g[i]
        mod_x = mods[i, :bsz].reshape(bsz, 6, d)
        mod_c = mods[i, bsz:bsz + 1].reshape(1, 6, d)
        if i % 2 == 0:
            ew = _even_weights(at_w_in[j], ga_q_norm[j], ga_k_norm[j], ml_q_norm[j], ml_kv_norm[j],
                               ml_w_q_up[j], ml_w_kv_up[j])
            px = _even_prep(x, mod_x, g, ew, rope_x, tm)
            pc = _even_prep(ctx, mod_c, g, ew, rope_c, tm)
            o1x, o2x = _attention(px[0:4:3], [px[1:3] + px[4:6], pc[1:3] + pc[4:6]], 512)
            if need_ctx:
                o1c, o2c = _attention(pc[0:4:3], [pc[1:3] + pc[4:6]], 256)
            w_out = at_w_out[j].astype(BF16)
        else:
            pad = jnp.zeros((d, LANE - ODD_DT), F32)
            win = jnp.concatenate([lc_w_in[j], pad], axis=1).astype(BF16)
            dtb = jnp.pad(mb_dt_bias[j].reshape(-1), (0, LANE - ODD_DT)).reshape(1, LANE)
            hy_x, z_x, xbc_x, dt_x = _odd_prep(x, mod_x, g, win, dtb, 2 * tm)
            hy_c, z_c, xbc_c, dt_c = _odd_prep(ctx, mod_c, g, win, dtb, tm)
            filt = (hy_w1[j], hy_b1[j], hy_w2[j], hy_b2[j], hy_w3[j], hy_freq[j])
            sw, sb = hy_short_w[j], hy_short_b[j].reshape(1, -1)
            o1x = _hyena(hy_x, sw, sb, hy_skip[j], filt)
            alog = jnp.pad(mb_a_log[j].reshape(-1), (0, LANE - ODD_DT)).reshape(1, LANE)
            dskip = jnp.repeat(mb_d_skip[j], D_HEAD_DIM).reshape(1, D_INNER)
            o2c, o2x = _ssd((xbc_c, dt_c, z_c), (xbc_x, dt_x, z_x), mb_conv_w[j], mb_conv_b[j].reshape(1, -1),
                            alog, dskip, mb_norm[j].reshape(1, -1), need_ctx)
            if need_ctx:
                o1c = _hyena(hy_c, sw, sb, hy_skip[j], filt)
            w_out = lc_w_out[j].astype(BF16)
        w1, w2 = ff_w1[i].astype(BF16), ff_w2[i].astype(BF16)
        x = _post(x, o1x, o2x, mod_x, g, w_out[:MIX_HALF], w_out[MIX_HALF:], w1, w2, tm, i > 0)
        if need_ctx:
            ctx = _post(ctx, o1c, o2c, mod_c, g, w_out[:MIX_HALF], w_out[MIX_HALF:], w1, w2, tm, i > 0)
    return x
```
